```python
import jax, jax.numpy as jnp
from jax import lax
import numpy as np

D_MODEL = 1024
BATCH = 8
SEQ = 4096
DEPTH = 1
DEC_BATCH = 2
DEC_SEQ = 8192
PAST_LEN = 128

GRID_W = 64
MIX_WIDTH = D_MODEL
ATTN_WIDTH = MIX_WIDTH // 2
FOURIER_WIDTH = MIX_WIDTH - ATTN_WIDTH
HEAD_DIM = 64
N_Q_HEADS = ATTN_WIDTH // HEAD_DIM
N_KV_HEADS = 2
GQA_GROUP = N_Q_HEADS // N_KV_HEADS
KV_WIDTH = N_KV_HEADS * HEAD_DIM
N_FOURIER_GROUPS = 4
FOURIER_GROUP_DIM = FOURIER_WIDTH // N_FOURIER_GROUPS
ROPE_THETA = 10000.0
Q_BLOCK = 128
EPS = 1e-6
SPLITS = [ATTN_WIDTH,
          ATTN_WIDTH + KV_WIDTH,
          ATTN_WIDTH + 2 * KV_WIDTH,
          2 * ATTN_WIDTH + 2 * KV_WIDTH,
          2 * ATTN_WIDTH + 2 * KV_WIDTH + FOURIER_WIDTH]
IN_WIDTH = 2 * ATTN_WIDTH + 2 * KV_WIDTH + 2 * FOURIER_WIDTH

kernel_name = "hymba_gqa_axialrope_fnet_encoder"


def _rmsnorm(x, w):
    xf = x.astype(jnp.float32)
    y = xf * lax.rsqrt(jnp.mean(xf * xf, axis=-1, keepdims=True) + EPS)
    return (y * w.astype(jnp.float32)).astype(x.dtype)


def _axial_rope_angles(seq_len):
    rows = seq_len // GRID_W
    row_idx, col_idx = jnp.meshgrid(jnp.arange(rows), jnp.arange(GRID_W), indexing="ij")
    row_idx = row_idx.reshape(-1).astype(jnp.float32)
    col_idx = col_idx.reshape(-1).astype(jnp.float32)
    axis_dim = HEAD_DIM // 2
    inv_freq = ROPE_THETA ** (-jnp.arange(0, axis_dim, 2, dtype=jnp.float32) / axis_dim)
    ang = jnp.concatenate([row_idx[:, None] * inv_freq, col_idx[:, None] * inv_freq], axis=-1)
    return jnp.cos(ang), jnp.sin(ang)


def _apply_rope(x, cos, sin):
    xf = x.astype(jnp.float32).reshape(*x.shape[:-1], HEAD_DIM // 2, 2)
    x1, x2 = xf[..., 0], xf[..., 1]
    c = cos[None, :, None, :]
    s = sin[None, :, None, :]
    out = jnp.stack([x1 * c - x2 * s, x1 * s + x2 * c], axis=-1).reshape(x.shape)
    return out.astype(x.dtype)


def _block_attention(q, k, v):
    B, S = q.shape[0], q.shape[1]
    nblk = S // Q_BLOCK
    qb = q.reshape(B, nblk, Q_BLOCK, N_KV_HEADS, GQA_GROUP, HEAD_DIM).transpose(1, 0, 2, 3, 4, 5)
    scale = HEAD_DIM ** -0.5

    def one_block(q_blk):
        s = jnp.einsum("bqkgd,bskd->bkgqs", q_blk, k, preferred_element_type=jnp.float32) * scale
        p = jax.nn.softmax(s, axis=-1).astype(v.dtype)
        return jnp.einsum("bkgqs,bskd->bqkgd", p, v)

    o = lax.map(one_block, qb)
    return o.transpose(1, 0, 2, 3, 4, 5).reshape(B, S, ATTN_WIDTH)


def _fourier_mix(u, w_f, b_f):
    B, S = u.shape[0], u.shape[1]
    ug = u.astype(jnp.float32).reshape(B, S, N_FOURIER_GROUPS, FOURIER_GROUP_DIM)
    mixed = jnp.real(jnp.fft.fft2(ug, axes=(1, 3), norm="ortho")).astype(u.dtype)
    out = jnp.einsum("bsgc,gcd->bsgd", mixed, w_f) + b_f
    return out.reshape(B, S, FOURIER_WIDTH)


def _mixer_layer(x, ln_w, w_in, q_norm, k_norm, w_f, b_f, w_out):
    B, S = x.shape[0], x.shape[1]
    h = _rmsnorm(x, ln_w)
    proj = jnp.einsum("bsd,de->bse", h, w_in)
    q, k, v, g_a, u_f, g_f = jnp.split(proj, SPLITS, axis=-1)
    cos, sin = _axial_rope_angles(S)
    q = _apply_rope(_rmsnorm(q.reshape(B, S, N_Q_HEADS, HEAD_DIM), q_norm), cos, sin)
    k = _apply_rope(_rmsnorm(k.reshape(B, S, N_KV_HEADS, HEAD_DIM), k_norm), cos, sin)
    v = v.reshape(B, S, N_KV_HEADS, HEAD_DIM)
    y_attn = _block_attention(q, k, v) * jax.nn.silu(g_a)
    y_four = _fourier_mix(u_f, w_f, b_f) * jax.nn.silu(g_f)
    y = jnp.concatenate([y_attn, y_four], axis=-1)
    return x + jnp.einsum("bse,ed->bsd", y, w_out)


def _trunk(x, ln_w, w_in, q_norm, k_norm, w_fourier, b_fourier, w_out, final_norm):
    for l in range(DEPTH):
        x = _mixer_layer(x, ln_w[l], w_in[l], q_norm[l], k_norm[l], w_fourier[l], b_fourier[l], w_out[l])
    return _rmsnorm(x, final_norm)


def setup_inputs(seed: int = 0) -> dict:
    key = jax.random.key(seed)
    ks = jax.random.split(key, 10)
    f32 = jnp.float32
    x_prompt = jax.random.normal(ks[0], (BATCH, SEQ, D_MODEL), f32)
    x_sample = jax.random.normal(ks[1], (DEC_BATCH, DEC_SEQ, D_MODEL), f32)
    ln_w = 1.0 + 0.02 * jax.random.normal(ks[2], (DEPTH, D_MODEL), f32)
    w_in = jax.random.normal(ks[3], (DEPTH, D_MODEL, IN_WIDTH), f32) * D_MODEL ** -0.5
    q_norm = 1.0 + 0.02 * jax.random.normal(ks[4], (DEPTH, HEAD_DIM), f32)
    k_norm = 1.0 + 0.02 * jax.random.normal(ks[5], (DEPTH, HEAD_DIM), f32)
    w_fourier = jax.random.normal(ks[6], (DEPTH, N_FOURIER_GROUPS, FOURIER_GROUP_DIM, FOURIER_GROUP_DIM), f32) * FOURIER_GROUP_DIM ** -0.5
    b_fourier = 0.02 * jax.random.normal(ks[7], (DEPTH, N_FOURIER_GROUPS, FOURIER_GROUP_DIM), f32)
    w_out = jax.random.normal(ks[8], (DEPTH, MIX_WIDTH, D_MODEL), f32) * MIX_WIDTH ** -0.5
    final_norm = 1.0 + 0.02 * jax.random.normal(ks[9], (D_MODEL,), f32)
    return {"x_prompt": x_prompt, "x_sample": x_sample, "ln_w": ln_w, "w_in": w_in,
            "q_norm": q_norm, "k_norm": k_norm, "w_fourier": w_fourier, "b_fourier": b_fourier,
            "w_out": w_out, "final_norm": final_norm}


def reference(x_prompt, x_sample, ln_w, w_in, q_norm, k_norm, w_fourier, b_fourier, w_out, final_norm):
    y_prompt = _trunk(x_prompt, ln_w, w_in, q_norm, k_norm, w_fourier, b_fourier, w_out, final_norm)
    y_sample = _trunk(x_sample, ln_w, w_in, q_norm, k_norm, w_fourier, b_fourier, w_out, final_norm)
    return (y_prompt, y_sample)
```

```python
import functools
import math

import jax
import jax.numpy as jnp
import numpy as np
from jax import lax
from jax.experimental import pallas as pl
from jax.experimental.pallas import tpu as pltpu

D_MODEL = 1024
GRID_W = 64
ATTN_WIDTH = 512
FOURIER_WIDTH = 512
HEAD_DIM = 64
N_Q_HEADS = 8
N_KV_HEADS = 2
GQA_GROUP = 4
KV_WIDTH = 128
N_FOURIER_GROUPS = 4
FOURIER_GROUP_DIM = 128
ROPE_THETA = 10000.0
EPS = 1e-6
IN_WIDTH = 2304
Q0, K0, V0, GA0, U0, GF0 = 0, 512, 640, 768, 1280, 1792

VT_ROWS = HEAD_DIM + 16

ROW_TILE = 512
Q_TILE = 256
VMEM_LIMIT_BYTES = 48 * 1024 * 1024

F32 = jnp.float32
BF16 = jnp.bfloat16


def _params(semantics):
    return pltpu.CompilerParams(dimension_semantics=semantics, vmem_limit_bytes=VMEM_LIMIT_BYTES)


def _rope(x, cos, sin_signed):
    n = x.shape[-1]
    lane_is_even = (lax.broadcasted_iota(jnp.int32, x.shape, 1) & 1) == 0
    partner = jnp.where(lane_is_even, pltpu.roll(x, n - 1, 1), pltpu.roll(x, 1, 1))
    return x * cos + partner * sin_signed


def _proj_kernel(x_ref, lnw_ref, win_ref, gq_ref, gk_ref, cos_ref, sin_ref, hmean_ref, cs_ref,
                 qT_ref, k_ref, vT_ref, gate_ref, ab_ref):
    x = x_ref[0]
    ms = jnp.mean(x * x, axis=-1, keepdims=True)
    h = (x * lax.rsqrt(ms + EPS) * lnw_ref[...]).astype(BF16)
    proj = jnp.dot(h, win_ref[...], preferred_element_type=F32)
    tm = proj.shape[0]

    cos = cos_ref[...]
    sin = sin_ref[...]
    hmean = hmean_ref[...]

    q = proj[:, Q0:Q0 + ATTN_WIDTH]
    q_ms = jnp.dot((q * q).astype(BF16), hmean, preferred_element_type=F32)
    q = q * lax.rsqrt(q_ms + EPS) * gq_ref[...]
    q = _rope(q, cos, sin)
    qT_ref[0] = q.T.astype(BF16)

    k = proj[:, K0:K0 + KV_WIDTH]
    k_ms = jnp.dot((k * k).astype(BF16), hmean[:KV_WIDTH, :KV_WIDTH], preferred_element_type=F32)
    k = k * lax.rsqrt(k_ms + EPS) * gk_ref[...]
    k = _rope(k, cos[:, :KV_WIDTH], sin[:, :KV_WIDTH]).astype(BF16)
    for j in range(N_KV_HEADS):
        k_ref[0, j] = k[:, j * HEAD_DIM:(j + 1) * HEAD_DIM]

    vT = proj[:, V0:V0 + KV_WIDTH].T.astype(BF16)
    ones = jnp.ones((VT_ROWS - HEAD_DIM, tm), BF16)
    for j in range(N_KV_HEADS):
        vT_ref[0, j, 0, :HEAD_DIM, :] = vT[j * HEAD_DIM:(j + 1) * HEAD_DIM]
        vT_ref[0, j, 0, HEAD_DIM:, :] = ones

    gate_ref[0, :, :ATTN_WIDTH] = jax.nn.silu(proj[:, GA0:GA0 + ATTN_WIDTH]).astype(BF16)
    gate_ref[0, :, ATTN_WIDTH:] = jax.nn.silu(proj[:, GF0:GF0 + FOURIER_WIDTH]).astype(BF16)

    cs = cs_ref[...]
    for g in range(N_FOURIER_GROUPS):
        lo = g * FOURIER_GROUP_DIM
        u = proj[:, U0 + lo:U0 + lo + FOURIER_GROUP_DIM].astype(BF16)
        ab = jnp.dot(u, cs, preferred_element_type=F32)
        ab_ref[0, :, lo:lo + FOURIER_GROUP_DIM] = ab[:, :FOURIER_GROUP_DIM].astype(BF16)
        ab_ref[1, :, lo:lo + FOURIER_GROUP_DIM] = ab[:, FOURIER_GROUP_DIM:].astype(BF16)


def _proj_call(x, lnw, win, gq, gk, cos_t, sin_t, hmean, cs):
    B, S, _ = x.shape
    tm = ROW_TILE
    n_t = S // tm
    const = lambda shape: pl.BlockSpec(shape, lambda b, i: (0,) * len(shape))
    return pl.pallas_call(
        _proj_kernel,
        grid=(B, n_t),
        in_specs=[
            pl.BlockSpec((1, tm, D_MODEL), lambda b, i: (b, i, 0)),
            const((1, D_MODEL)),
            const((D_MODEL, IN_WIDTH)),
            const((1, ATTN_WIDTH)),
            const((1, KV_WIDTH)),
            pl.BlockSpec((tm, ATTN_WIDTH), lambda b, i: (i, 0)),
            pl.BlockSpec((tm, ATTN_WIDTH), lambda b, i: (i, 0)),
            const((ATTN_WIDTH, ATTN_WIDTH)),
            const((FOURIER_GROUP_DIM, 2 * FOURIER_GROUP_DIM)),
        ],
        out_specs=[
            pl.BlockSpec((1, ATTN_WIDTH, tm), lambda b, i: (b, 0, i)),
            pl.BlockSpec((1, N_KV_HEADS, tm, HEAD_DIM), lambda b, i: (b, 0, i, 0)),
            pl.BlockSpec((1, N_KV_HEADS, 1, VT_ROWS, tm), lambda b, i: (b, 0, i, 0, 0)),
            pl.BlockSpec((1, tm, 2 * ATTN_WIDTH), lambda b, i: (b, i, 0)),
            pl.BlockSpec((2, tm, FOURIER_WIDTH), lambda b, i: (0, i, b)),
        ],
        out_shape=[
            jax.ShapeDtypeStruct((B, ATTN_WIDTH, S), BF16),
            jax.ShapeDtypeStruct((B, N_KV_HEADS, S, HEAD_DIM), BF16),
            jax.ShapeDtypeStruct((B, N_KV_HEADS, n_t, VT_ROWS, tm), BF16),
            jax.ShapeDtypeStruct((B, S, 2 * ATTN_WIDTH), BF16),
            jax.ShapeDtypeStruct((2, S, B * FOURIER_WIDTH), BF16),
        ],
        compiler_params=_params(("parallel", "parallel")),
        name="proj",
    )(x, lnw, win, gq, gk, cos_t, sin_t, hmean, cs)


def _attn_kernel(qT_ref, k_ref, vT_ref, o_ref, *, n_chunks, tk):
    tq = qT_ref.shape[-1]
    outs = []
    for hh in range(GQA_GROUP):
        qT = qT_ref[0, hh * HEAD_DIM:(hh + 1) * HEAD_DIM, :]

        def body(c, carry, qT=qT):
            m, acc = carry
            off = pl.multiple_of(c * tk, tk)
            kc = k_ref[0, 0, pl.ds(off, tk), :]
            s = jnp.dot(kc, qT, preferred_element_type=F32)
            m_new = jnp.maximum(m, jnp.max(s, axis=0, keepdims=True))
            p = jnp.exp2(s - m_new).astype(BF16)
            alpha = jnp.exp2(m - m_new)
            pv = jnp.dot(vT_ref[0, 0, c], p, preferred_element_type=F32)
            return m_new, alpha * acc + pv

        m0 = jnp.full((1, tq), -jnp.inf, F32)
        acc0 = jnp.zeros((VT_ROWS, tq), F32)
        _, acc = lax.fori_loop(0, n_chunks, body, (m0, acc0))
        outs.append(acc[:HEAD_DIM] / acc[HEAD_DIM:HEAD_DIM + 1])
    oT = jnp.concatenate(outs, axis=0)
    o_ref[0] = oT.T.astype(BF16)


def _attn_call(qT, k, vT):
    B, _, S = qT.shape
    n_chunks, tk = vT.shape[2], vT.shape[4]
    tq = Q_TILE
    width = GQA_GROUP * HEAD_DIM
    return pl.pallas_call(
        functools.partial(_attn_kernel, n_chunks=n_chunks, tk=tk),
        grid=(B, N_KV_HEADS, S // tq),
        in_specs=[
            pl.BlockSpec((1, width, tq), lambda b, j, i: (b, j, i)),
            pl.BlockSpec((1, 1, S, HEAD_DIM), lambda b, j, i: (b, j, 0, 0)),
            pl.BlockSpec((1, 1, n_chunks, VT_ROWS, tk), lambda b, j, i: (b, j, 0, 0, 0)),
        ],
        out_specs=pl.BlockSpec((1, tq, width), lambda b, j, i: (b, i, j)),
        out_shape=jax.ShapeDtypeStruct((B, S, ATTN_WIDTH), BF16),
        compiler_params=_params(("parallel", "parallel", "parallel")),
        name="attn",
    )(qT, k, vT)


def _dft_kernel(d_ref, ab_ref, o_ref, acc_ref, *, scale):
    kk = pl.program_id(2)

    @pl.when(kk == 0)
    def _():
        acc_ref[...] = jnp.zeros_like(acc_ref)

    acc_ref[...] += jnp.dot(d_ref[...], ab_ref[...], preferred_element_type=F32)

    @pl.when(kk == pl.num_programs(2) - 1)
    def _():
        o_ref[...] = (acc_ref[...] * scale).astype(o_ref.dtype)


def _dft_call(dmat, ab, scale):
    S, K = dmat.shape
    N = ab.shape[1]
    tm, tn, tk = 1024, 1024, 1024
    return pl.pallas_call(
        functools.partial(_dft_kernel, scale=scale),
        grid=(S // tm, N // tn, K // tk),
        in_specs=[
            pl.BlockSpec((tm, tk), lambda i, j, k: (i, k)),
            pl.BlockSpec((tk, tn), lambda i, j, k: (k, j)),
        ],
        out_specs=pl.BlockSpec((tm, tn), lambda i, j, k: (i, j)),
        out_shape=jax.ShapeDtypeStruct((S, N), BF16),
        scratch_shapes=[pltpu.VMEM((tm, tn), F32)],
        compiler_params=_params(("parallel", "parallel", "arbitrary")),
        name="pos_dft",
    )(dmat, ab)


def _out_kernel(x_ref, ya_ref, gate_ref, mix_ref, wf_ref, bf_ref, wout_ref, fn_ref, o_ref):
    gate_a = gate_ref[0, :, :ATTN_WIDTH].astype(F32)
    gate_f = gate_ref[0, :, ATTN_WIDTH:].astype(F32)
    ya = (ya_ref[0].astype(F32) * gate_a).astype(BF16)
    mix = mix_ref[...]
    parts = []
    for g in range(N_FOURIER_GROUPS):
        lo = g * FOURIER_GROUP_DIM
        parts.append(jnp.dot(mix[:, lo:lo + FOURIER_GROUP_DIM], wf_ref[g], preferred_element_type=F32))
    yf = ((jnp.concatenate(parts, axis=-1) + bf_ref[...]) * gate_f).astype(BF16)
    out = x_ref[0]
    out = out + jnp.dot(ya, wout_ref[:ATTN_WIDTH, :], preferred_element_type=F32)
    out = out + jnp.dot(yf, wout_ref[ATTN_WIDTH:, :], preferred_element_type=F32)
    ms = jnp.mean(out * out, axis=-1, keepdims=True)
    o_ref[0] = out * lax.rsqrt(ms + EPS) * fn_ref[...]


def _out_call(x, ya, gates, mixed, wf, bf, wout, fnorm):
    B, S, _ = x.shape
    tm = ROW_TILE
    const = lambda shape: pl.BlockSpec(shape, lambda b, i: (0,) * len(shape))
    return pl.pallas_call(
        _out_kernel,
        grid=(B, S // tm),
        in_specs=[
            pl.BlockSpec((1, tm, D_MODEL), lambda b, i: (b, i, 0)),
            pl.BlockSpec((1, tm, ATTN_WIDTH), lambda b, i: (b, i, 0)),
            pl.BlockSpec((1, tm, 2 * ATTN_WIDTH), lambda b, i: (b, i, 0)),
            pl.BlockSpec((tm, FOURIER_WIDTH), lambda b, i: (i, b)),
            const((N_FOURIER_GROUPS, FOURIER_GROUP_DIM, FOURIER_GROUP_DIM)),
            const((1, FOURIER_WIDTH)),
            const((D_MODEL, D_MODEL)),
            const((1, D_MODEL)),
        ],
        out_specs=pl.BlockSpec((1, tm, D_MODEL), lambda b, i: (b, i, 0)),
        out_shape=jax.ShapeDtypeStruct((B, S, D_MODEL), F32),
        compiler_params=_params(("parallel", "parallel")),
        name="out_proj",
    )(x, ya, gates, mixed, wf, bf, wout, fnorm)


def _rope_tables(seq_len):
    rows = seq_len // GRID_W
    row_idx, col_idx = jnp.meshgrid(jnp.arange(rows), jnp.arange(GRID_W), indexing="ij")
    row_idx = row_idx.reshape(-1).astype(F32)
    col_idx = col_idx.reshape(-1).astype(F32)
    axis_dim = HEAD_DIM // 2
    inv_freq = ROPE_THETA ** (-jnp.arange(0, axis_dim, 2, dtype=F32) / axis_dim)
    ang = jnp.concatenate([row_idx[:, None] * inv_freq, col_idx[:, None] * inv_freq], axis=-1)
    cos, sin = jnp.cos(ang), jnp.sin(ang)
    cos_pair = jnp.repeat(cos, 2, axis=-1)
    sin_pair = jnp.stack([-sin, sin], axis=-1).reshape(seq_len, HEAD_DIM)
    return jnp.tile(cos_pair, (1, N_Q_HEADS)), jnp.tile(sin_pair, (1, N_Q_HEADS))


def _dft_cos_sin(n):
    idx = jnp.arange(n, dtype=jnp.int32)
    ang = ((idx[:, None] * idx[None, :]) % n).astype(F32) * (2.0 * math.pi / n)
    return jnp.cos(ang), jnp.sin(ang)


def _trunk(x, lnw, win, gq, gk, hmean, cs, wf, bf, wout, fnorm):
    B, S, _ = x.shape
    cos_t, sin_t = _rope_tables(S)
    qT, k, vT, gates, ab = _proj_call(x, lnw, win, gq, gk, cos_t, sin_t, hmean, cs)
    ya = _attn_call(qT, k, vT)
    c_s, s_s = _dft_cos_sin(S)
    dmat = jnp.concatenate([c_s, -s_s], axis=1).astype(BF16)
    scale = 1.0 / math.sqrt(S * FOURIER_GROUP_DIM)
    mixed = _dft_call(dmat, ab.reshape(2 * S, B * FOURIER_WIDTH), scale)
    return _out_call(x, ya, gates, mixed, wf, bf, wout, fnorm)


def kernel(x_prompt, x_sample, ln_w, w_in, q_norm, k_norm, w_fourier, b_fourier, w_out, final_norm):
    assert ln_w.shape[0] == 1, "single mixer layer"
    lnw = ln_w[0].reshape(1, D_MODEL)
    win = w_in[0].astype(BF16)
    q_scale = HEAD_DIM ** -0.5 * math.log2(math.e)
    gq = jnp.tile(q_norm[0] * q_scale, N_Q_HEADS).reshape(1, ATTN_WIDTH)
    gk = jnp.tile(k_norm[0], N_KV_HEADS).reshape(1, KV_WIDTH)
    head_id = np.arange(ATTN_WIDTH) // HEAD_DIM
    hmean = jnp.asarray((head_id[:, None] == head_id[None, :]) / HEAD_DIM, BF16)
    c_c, s_c = _dft_cos_sin(FOURIER_GROUP_DIM)
    cs = jnp.concatenate([c_c, s_c], axis=1).astype(BF16)
    wf = w_fourier[0].astype(BF16)
    bf = b_fourier[0].reshape(1, FOURIER_WIDTH)
    wout = w_out[0].astype(BF16)
    fnorm = final_norm.reshape(1, D_MODEL)
    args = (lnw, win, gq, gk, hmean, cs, wf, bf, wout, fnorm)
    return (_trunk(x_prompt, *args), _trunk(x_sample, *args))
```

```python
import functools
import math

import jax
import jax.numpy as jnp
import numpy as np
from jax import lax
from jax.experimental import pallas as pl
from jax.experimental.pallas import tpu as pltpu

D_MODEL = 1024
GRID_W = 64
ATTN_WIDTH = 512
FOURIER_WIDTH = 512
HEAD_DIM = 64
N_Q_HEADS = 8
N_KV_HEADS = 2
GQA_GROUP = 4
KV_WIDTH = 128
N_FOURIER_GROUPS = 4
FOURIER_GROUP_DIM = 128
ROPE_THETA = 10000.0
EPS = 1e-6
IN_WIDTH = 2304
Q0, K0, V0, GA0, U0, GF0 = 0, 512, 640, 768, 1280, 1792

VT_ROWS = HEAD_DIM + 16

ROW_TILE = 512
Q_TILE = 256
VMEM_LIMIT_BYTES = 48 * 1024 * 1024

F32 = jnp.float32
BF16 = jnp.bfloat16


def _params(semantics):
    return pltpu.CompilerParams(dimension_semantics=semantics, vmem_limit_bytes=VMEM_LIMIT_BYTES)


def _rope(x, cos, sin_signed):
    n = x.shape[-1]
    lane_is_even = (lax.broadcasted_iota(jnp.int32, x.shape, 1) & 1) == 0
    partner = jnp.where(lane_is_even, pltpu.roll(x, n - 1, 1), pltpu.roll(x, 1, 1))
    return x * cos + partner * sin_signed


def _proj_kernel(x_ref, lnw_ref, win_ref, gq_ref, gk_ref, cos_ref, sin_ref, hmean_ref, cs_ref,
                 qT_ref, k_ref, vT_ref, gate_ref, ab_ref):
    x = x_ref[0]
    ms = jnp.mean(x * x, axis=-1, keepdims=True)
    h = (x * lax.rsqrt(ms + EPS) * lnw_ref[...]).astype(BF16)
    proj = jnp.dot(h, win_ref[...], preferred_element_type=F32)
    tm = proj.shape[0]

    cos = cos_ref[...]
    sin = sin_ref[...]
    hmean = hmean_ref[...]

    q = proj[:, Q0:Q0 + ATTN_WIDTH]
    q_ms = jnp.dot((q * q).astype(BF16), hmean, preferred_element_type=F32)
    q = q * lax.rsqrt(q_ms + EPS) * gq_ref[...]
    q = _rope(q, cos, sin)
    qT = q.T.astype(BF16)
    for hd in range(N_Q_HEADS):
        j, hh = divmod(hd, GQA_GROUP)
        for t in range(tm // Q_TILE):
            qT_ref[0, j, t, :, hh * Q_TILE:(hh + 1) * Q_TILE] = (
                qT[hd * HEAD_DIM:(hd + 1) * HEAD_DIM, t * Q_TILE:(t + 1) * Q_TILE])

    k = proj[:, K0:K0 + KV_WIDTH]
    k_ms = jnp.dot((k * k).astype(BF16), hmean[:KV_WIDTH, :KV_WIDTH], preferred_element_type=F32)
    k = k * lax.rsqrt(k_ms + EPS) * gk_ref[...]
    k = _rope(k, cos[:, :KV_WIDTH], sin[:, :KV_WIDTH]).astype(BF16)
    for j in range(N_KV_HEADS):
        k_ref[0, j] = k[:, j * HEAD_DIM:(j + 1) * HEAD_DIM]

    vT = proj[:, V0:V0 + KV_WIDTH].T.astype(BF16)
    ones = jnp.ones((VT_ROWS - HEAD_DIM, tm), BF16)
    for j in range(N_KV_HEADS):
        vT_ref[0, j, 0, :HEAD_DIM, :] = vT[j * HEAD_DIM:(j + 1) * HEAD_DIM]
        vT_ref[0, j, 0, HEAD_DIM:, :] = ones

    gate_ref[0, :, :ATTN_WIDTH] = jax.nn.silu(proj[:, GA0:GA0 + ATTN_WIDTH]).astype(BF16)
    gate_ref[0, :, ATTN_WIDTH:] = jax.nn.silu(proj[:, GF0:GF0 + FOURIER_WIDTH]).astype(BF16)

    cs = cs_ref[...]
    for g in range(N_FOURIER_GROUPS):
        lo = g * FOURIER_GROUP_DIM
        u = proj[:, U0 + lo:U0 + lo + FOURIER_GROUP_DIM].astype(BF16)
        ab = jnp.dot(u, cs, preferred_element_type=F32)
        ab_ref[0, :, lo:lo + FOURIER_GROUP_DIM] = ab[:, :FOURIER_GROUP_DIM].astype(BF16)
        ab_ref[1, :, lo:lo + FOURIER_GROUP_DIM] = ab[:, FOURIER_GROUP_DIM:].astype(BF16)


def _proj_call(x, lnw, win, gq, gk, cos_t, sin_t, hmean, cs):
    B, S, _ = x.shape
    tm = ROW_TILE
    n_t = S // tm
    const = lambda shape: pl.BlockSpec(shape, lambda b, i: (0,) * len(shape))
    return pl.pallas_call(
        _proj_kernel,
        grid=(B, n_t),
        in_specs=[
            pl.BlockSpec((1, tm, D_MODEL), lambda b, i: (b, i, 0)),
            const((1, D_MODEL)),
            const((D_MODEL, IN_WIDTH)),
            const((1, ATTN_WIDTH)),
            const((1, KV_WIDTH)),
            pl.BlockSpec((tm, ATTN_WIDTH), lambda b, i: (i, 0)),
            pl.BlockSpec((tm, ATTN_WIDTH), lambda b, i: (i, 0)),
            const((ATTN_WIDTH, ATTN_WIDTH)),
            const((FOURIER_GROUP_DIM, 2 * FOURIER_GROUP_DIM)),
        ],
        out_specs=[
            pl.BlockSpec((1, N_KV_HEADS, tm // Q_TILE, HEAD_DIM, GQA_GROUP * Q_TILE), lambda b, i: (b, 0, i, 0, 0)),
            pl.BlockSpec((1, N_KV_HEADS, tm, HEAD_DIM), lambda b, i: (b, 0, i, 0)),
            pl.BlockSpec((1, N_KV_HEADS, 1, VT_ROWS, tm), lambda b, i: (b, 0, i, 0, 0)),
            pl.BlockSpec((1, tm, 2 * ATTN_WIDTH), lambda b, i: (b, i, 0)),
            pl.BlockSpec((2, tm, FOURIER_WIDTH), lambda b, i: (0, i, b)),
        ],
        out_shape=[
            jax.ShapeDtypeStruct((B, N_KV_HEADS, S // Q_TILE, HEAD_DIM, GQA_GROUP * Q_TILE), BF16),
            jax.ShapeDtypeStruct((B, N_KV_HEADS, S, HEAD_DIM), BF16),
            jax.ShapeDtypeStruct((B, N_KV_HEADS, n_t, VT_ROWS, tm), BF16),
            jax.ShapeDtypeStruct((B, S, 2 * ATTN_WIDTH), BF16),
            jax.ShapeDtypeStruct((2, S, B * FOURIER_WIDTH), BF16),
        ],
        compiler_params=_params(("parallel", "parallel")),
        name="proj",
    )(x, lnw, win, gq, gk, cos_t, sin_t, hmean, cs)


def _attn_kernel(qT_ref, k_ref, vT_ref, o_ref, m_ref, acc_ref, s_ref, cmax_ref, *, n_chunks, tk):
    tq = o_ref.shape[1]
    qT = qT_ref[0, 0, 0]
    m_ref[...] = jnp.full(m_ref.shape, -jnp.inf, F32)
    acc_ref[...] = jnp.zeros(acc_ref.shape, F32)

    def scores(c, slot):
        off = pl.multiple_of(c * tk, tk)
        s = jnp.dot(k_ref[0, 0, pl.ds(off, tk), :], qT, preferred_element_type=F32)
        s_ref[slot] = s
        cmax_ref[slot] = jnp.max(s, axis=0, keepdims=True)

    def accumulate(c, slot):
        m_old = m_ref[...]
        m_new = jnp.maximum(m_old, cmax_ref[slot])
        p = jnp.exp2(s_ref[slot] - m_new).astype(BF16)
        alpha = jnp.exp2(m_old - m_new)
        pv = jnp.dot(vT_ref[0, 0, c], p, preferred_element_type=F32)
        acc_ref[...] = alpha * acc_ref[...] + pv
        m_ref[...] = m_new

    scores(0, 0)

    def pair(i, carry):
        c = 2 * i
        scores(c + 1, 1)
        accumulate(c, 0)
        scores(c + 2, 0)
        accumulate(c + 1, 1)
        return carry

    lax.fori_loop(0, n_chunks // 2 - 1, pair, 0)
    scores(n_chunks - 1, 1)
    accumulate(n_chunks - 2, 0)
    accumulate(n_chunks - 1, 1)
    acc = acc_ref[...]
    oT = acc[:HEAD_DIM] / acc[HEAD_DIM:HEAD_DIM + 1]
    oT = jnp.concatenate([oT[:, hh * tq:(hh + 1) * tq] for hh in range(GQA_GROUP)], axis=0)
    o_ref[0] = oT.T.astype(BF16)


def _attn_call(qT, k, vT):
    B, _, n_q, _, _ = qT.shape
    S = k.shape[2]
    n_chunks, tk = vT.shape[2], vT.shape[4]
    tq = Q_TILE
    width = GQA_GROUP * HEAD_DIM
    return pl.pallas_call(
        functools.partial(_attn_kernel, n_chunks=n_chunks, tk=tk),
        grid=(B, N_KV_HEADS, n_q),
        in_specs=[
            pl.BlockSpec((1, 1, 1, HEAD_DIM, GQA_GROUP * tq), lambda b, j, i: (b, j, i, 0, 0)),
            pl.BlockSpec((1, 1, S, HEAD_DIM), lambda b, j, i: (b, j, 0, 0)),
            pl.BlockSpec((1, 1, n_chunks, VT_ROWS, tk), lambda b, j, i: (b, j, 0, 0, 0)),
        ],
        out_specs=pl.BlockSpec((1, tq, width), lambda b, j, i: (b, i, j)),
        out_shape=jax.ShapeDtypeStruct((B, S, ATTN_WIDTH), BF16),
        scratch_shapes=[pltpu.VMEM((1, GQA_GROUP * tq), F32), pltpu.VMEM((VT_ROWS, GQA_GROUP * tq), F32),
                        pltpu.VMEM((2, tk, GQA_GROUP * tq), F32), pltpu.VMEM((2, 1, GQA_GROUP * tq), F32)],
        compiler_params=_params(("parallel", "parallel", "parallel")),
        name="attn",
    )(qT, k, vT)


def _dft_kernel(d_ref, ab_ref, o_ref, acc_ref, *, scale):
    kk = pl.program_id(2)

    @pl.when(kk == 0)
    def _():
        acc_ref[...] = jnp.zeros_like(acc_ref)

    acc_ref[...] += jnp.dot(d_ref[...], ab_ref[...], preferred_element_type=F32)

    @pl.when(kk == pl.num_programs(2) - 1)
    def _():
        o_ref[...] = (acc_ref[...] * scale).astype(o_ref.dtype)


def _dft_call(dmat, ab, scale):
    S, K = dmat.shape
    N = ab.shape[1]
    tm, tn, tk = 1024, 1024, 1024
    return pl.pallas_call(
        functools.partial(_dft_kernel, scale=scale),
        grid=(S // tm, N // tn, K // tk),
        in_specs=[
            pl.BlockSpec((tm, tk), lambda i, j, k: (i, k)),
            pl.BlockSpec((tk, tn), lambda i, j, k: (k, j)),
        ],
        out_specs=pl.BlockSpec((tm, tn), lambda i, j, k: (i, j)),
        out_shape=jax.ShapeDtypeStruct((S, N), BF16),
        scratch_shapes=[pltpu.VMEM((tm, tn), F32)],
        compiler_params=_params(("parallel", "parallel", "arbitrary")),
        name="pos_dft",
    )(dmat, ab)


def _out_kernel(x_ref, ya_ref, gate_ref, mix_ref, wf_ref, bf_ref, wout_ref, fn_ref, o_ref):
    gate_a = gate_ref[0, :, :ATTN_WIDTH].astype(F32)
    gate_f = gate_ref[0, :, ATTN_WIDTH:].astype(F32)
    ya = (ya_ref[0].astype(F32) * gate_a).astype(BF16)
    mix = mix_ref[...]
    parts = []
    for g in range(N_FOURIER_GROUPS):
        lo = g * FOURIER_GROUP_DIM
        parts.append(jnp.dot(mix[:, lo:lo + FOURIER_GROUP_DIM], wf_ref[g], preferred_element_type=F32))
    yf = ((jnp.concatenate(parts, axis=-1) + bf_ref[...]) * gate_f).astype(BF16)
    out = x_ref[0]
    out = out + jnp.dot(ya, wout_ref[:ATTN_WIDTH, :], preferred_element_type=F32)
    out = out + jnp.dot(yf, wout_ref[ATTN_WIDTH:, :], preferred_element_type=F32)
    ms = jnp.mean(out * out, axis=-1, keepdims=True)
    o_ref[0] = out * lax.rsqrt(ms + EPS) * fn_ref[...]


def _out_call(x, ya, gates, mixed, wf, bf, wout, fnorm):
    B, S, _ = x.shape
    tm = ROW_TILE
    const = lambda shape: pl.BlockSpec(shape, lambda b, i: (0,) * len(shape))
    return pl.pallas_call(
        _out_kernel,
        grid=(B, S // tm),
        in_specs=[
            pl.BlockSpec((1, tm, D_MODEL), lambda b, i: (b, i, 0)),
            pl.BlockSpec((1, tm, ATTN_WIDTH), lambda b, i: (b, i, 0)),
            pl.BlockSpec((1, tm, 2 * ATTN_WIDTH), lambda b, i: (b, i, 0)),
            pl.BlockSpec((tm, FOURIER_WIDTH), lambda b, i: (i, b)),
            const((N_FOURIER_GROUPS, FOURIER_GROUP_DIM, FOURIER_GROUP_DIM)),
            const((1, FOURIER_WIDTH)),
            const((D_MODEL, D_MODEL)),
            const((1, D_MODEL)),
        ],
        out_specs=pl.BlockSpec((1, tm, D_MODEL), lambda b, i: (b, i, 0)),
        out_shape=jax.ShapeDtypeStruct((B, S, D_MODEL), F32),
        compiler_params=_params(("parallel", "parallel")),
        name="out_proj",
    )(x, ya, gates, mixed, wf, bf, wout, fnorm)


def _rope_tables(seq_len):
    rows = seq_len // GRID_W
    row_idx, col_idx = jnp.meshgrid(jnp.arange(rows), jnp.arange(GRID_W), indexing="ij")
    row_idx = row_idx.reshape(-1).astype(F32)
    col_idx = col_idx.reshape(-1).astype(F32)
    axis_dim = HEAD_DIM // 2
    inv_freq = ROPE_THETA ** (-jnp.arange(0, axis_dim, 2, dtype=F32) / axis_dim)
    ang = jnp.concatenate([row_idx[:, None] * inv_freq, col_idx[:, None] * inv_freq], axis=-1)
    cos, sin = jnp.cos(ang), jnp.sin(ang)
    cos_pair = jnp.repeat(cos, 2, axis=-1)
    sin_pair = jnp.stack([-sin, sin], axis=-1).reshape(seq_len, HEAD_DIM)
    return jnp.tile(cos_pair, (1, N_Q_HEADS)), jnp.tile(sin_pair, (1, N_Q_HEADS))


def _dft_cos_sin(n):
    idx = jnp.arange(n, dtype=jnp.int32)
    ang = ((idx[:, None] * idx[None, :]) % n).astype(F32) * (2.0 * math.pi / n)
    return jnp.cos(ang), jnp.sin(ang)


def _trunk(x, lnw, win, gq, gk, hmean, cs, wf, bf, wout, fnorm):
    B, S, _ = x.shape
    cos_t, sin_t = _rope_tables(S)
    qT, k, vT, gates, ab = _proj_call(x, lnw, win, gq, gk, cos_t, sin_t, hmean, cs)
    ya = _attn_call(qT, k, vT)
    c_s, s_s = _dft_cos_sin(S)
    dmat = jnp.concatenate([c_s, -s_s], axis=1).astype(BF16)
    scale = 1.0 / math.sqrt(S * FOURIER_GROUP_DIM)
    mixed = _dft_call(dmat, ab.reshape(2 * S, B * FOURIER_WIDTH), scale)
    return _out_call(x, ya, gates, mixed, wf, bf, wout, fnorm)


def kernel(x_prompt, x_sample, ln_w, w_in, q_norm, k_norm, w_fourier, b_fourier, w_out, final_norm):
    assert ln_w.shape[0] == 1, "single mixer layer"
    lnw = ln_w[0].reshape(1, D_MODEL)
    win = w_in[0].astype(BF16)
    q_scale = HEAD_DIM ** -0.5 * math.log2(math.e)
    gq = jnp.tile(q_norm[0] * q_scale, N_Q_HEADS).reshape(1, ATTN_WIDTH)
    gk = jnp.tile(k_norm[0], N_KV_HEADS).reshape(1, KV_WIDTH)
    head_id = np.arange(ATTN_WIDTH) // HEAD_DIM
    hmean = jnp.asarray((head_id[:, None] == head_id[None, :]) / HEAD_DIM, BF16)
    c_c, s_c = _dft_cos_sin(FOURIER_GROUP_DIM)
    cs = jnp.concatenate([c_c, s_c], axis=1).astype(BF16)
    wf = w_fourier[0].astype(BF16)
    bf = b_fourier[0].reshape(1, FOURIER_WIDTH)
    wout = w_out[0].astype(BF16)
    fnorm = final_norm.reshape(1, D_MODEL)
    args = (lnw, win, gq, gk, hmean, cs, wf, bf, wout, fnorm)
    return (_trunk(x_prompt, *args), _trunk(x_sample, *args))
```

```python
import functools
import math

import jax
import jax.numpy as jnp
import numpy as np
from jax import lax
from jax.experimental import pallas as pl
from jax.experimental.pallas import tpu as pltpu

D_MODEL = 1024
GRID_W = 64
ATTN_WIDTH = 512
FOURIER_WIDTH = 512
HEAD_DIM = 64
N_Q_HEADS = 8
N_KV_HEADS = 2
GQA_GROUP = 4
KV_WIDTH = 128
N_FOURIER_GROUPS = 4
FOURIER_GROUP_DIM = 128
ROPE_THETA = 10000.0
EPS = 1e-6
IN_WIDTH = 2304
Q0, K0, V0, GA0, U0, GF0 = 0, 512, 640, 768, 1280, 1792

VT_ROWS = HEAD_DIM + 16

DFT_A = 64
TILE_J = 8
ROW_TILE = DFT_A * TILE_J
Q_TILE = 256
VMEM_LIMIT_BYTES = 48 * 1024 * 1024

F32 = jnp.float32
BF16 = jnp.bfloat16


def _params(semantics):
    return pltpu.CompilerParams(dimension_semantics=semantics, vmem_limit_bytes=VMEM_LIMIT_BYTES)


def _rope(x, cos, sin_signed):
    n = x.shape[-1]
    lane_is_even = (lax.broadcasted_iota(jnp.int32, x.shape, 1) & 1) == 0
    partner = jnp.where(lane_is_even, pltpu.roll(x, n - 1, 1), pltpu.roll(x, 1, 1))
    return x * cos + partner * sin_signed


def _proj_kernel(x_ref, lnw_ref, win_ref, gq_ref, gk_ref, cos_ref, sin_ref, hmean_ref, cs_ref, m1_ref,
                 qT_ref, k_ref, vT_ref, gate_ref, y_ref):
    tm = ROW_TILE
    x = x_ref[0].reshape(tm, D_MODEL)
    ms = jnp.mean(x * x, axis=-1, keepdims=True)
    h = (x * lax.rsqrt(ms + EPS) * lnw_ref[...]).astype(BF16)
    proj = jnp.dot(h, win_ref[...], preferred_element_type=F32)

    cos2 = cos_ref[...]
    sin2 = sin_ref[...]
    cos = jnp.concatenate([cos2] * (ATTN_WIDTH // KV_WIDTH), axis=1)
    sin = jnp.concatenate([sin2] * (ATTN_WIDTH // KV_WIDTH), axis=1)
    hmean = hmean_ref[...]

    q = proj[:, Q0:Q0 + ATTN_WIDTH]
    q_ms = jnp.dot((q * q).astype(BF16), hmean, preferred_element_type=F32)
    q = q * lax.rsqrt(q_ms + EPS) * gq_ref[...]
    q = _rope(q, cos, sin)
    qT = q.T.astype(BF16)
    for hd in range(N_Q_HEADS):
        j, hh = divmod(hd, GQA_GROUP)
        for t in range(tm // Q_TILE):
            qT_ref[0, j, t, :, hh * Q_TILE:(hh + 1) * Q_TILE] = (
                qT[hd * HEAD_DIM:(hd + 1) * HEAD_DIM, t * Q_TILE:(t + 1) * Q_TILE])

    k = proj[:, K0:K0 + KV_WIDTH]
    k_ms = jnp.dot((k * k).astype(BF16), hmean[:KV_WIDTH, :KV_WIDTH], preferred_element_type=F32)
    k = k * lax.rsqrt(k_ms + EPS) * gk_ref[...]
    k = _rope(k, cos2, sin2).astype(BF16)
    for j in range(N_KV_HEADS):
        k_ref[0, j] = k[:, j * HEAD_DIM:(j + 1) * HEAD_DIM]

    vT = proj[:, V0:V0 + KV_WIDTH].T.astype(BF16)
    ones = jnp.ones((VT_ROWS - HEAD_DIM, tm), BF16)
    for j in range(N_KV_HEADS):
        vT_ref[0, j, 0, :HEAD_DIM, :] = vT[j * HEAD_DIM:(j + 1) * HEAD_DIM]
        vT_ref[0, j, 0, HEAD_DIM:, :] = ones

    gate_ref[0, :, :ATTN_WIDTH] = jax.nn.silu(proj[:, GA0:GA0 + ATTN_WIDTH]).astype(BF16)
    gate_ref[0, :, ATTN_WIDTH:] = jax.nn.silu(proj[:, GF0:GF0 + FOURIER_WIDTH]).astype(BF16)

    cs = cs_ref[...]
    a_parts, b_parts = [], []
    for g in range(N_FOURIER_GROUPS):
        lo = g * FOURIER_GROUP_DIM
        u = proj[:, U0 + lo:U0 + lo + FOURIER_GROUP_DIM].astype(BF16)
        ab = jnp.dot(u, cs, preferred_element_type=F32)
        a_parts.append(ab[:, :FOURIER_GROUP_DIM].astype(BF16))
        b_parts.append(ab[:, FOURIER_GROUP_DIM:].astype(BF16))
    z = jnp.concatenate([jnp.concatenate(a_parts, axis=1), jnp.concatenate(b_parts, axis=1)], axis=0)
    y = jnp.dot(m1_ref[...], z, preferred_element_type=F32).astype(BF16)
    y_ref[0, 0, 0] = y[:tm]
    y_ref[0, 0, 1] = y[tm:]


def _proj_call(x4, lnw, win, gq, gk, cos_t, sin_t, hmean, cs, m1):
    B, _, S2, _ = x4.shape
    S = DFT_A * S2
    tm = ROW_TILE
    n_t = S // tm
    const = lambda shape: pl.BlockSpec(shape, lambda b, i: (0,) * len(shape))
    return pl.pallas_call(
        _proj_kernel,
        grid=(B, n_t),
        in_specs=[
            pl.BlockSpec((1, DFT_A, TILE_J, D_MODEL), lambda b, i: (b, 0, i, 0)),
            const((1, D_MODEL)),
            const((D_MODEL, IN_WIDTH)),
            const((1, ATTN_WIDTH)),
            const((1, KV_WIDTH)),
            pl.BlockSpec((tm, KV_WIDTH), lambda b, i: (i, 0)),
            pl.BlockSpec((tm, KV_WIDTH), lambda b, i: (i, 0)),
            const((ATTN_WIDTH, ATTN_WIDTH)),
            const((FOURIER_GROUP_DIM, 2 * FOURIER_GROUP_DIM)),
            const((2 * tm, 2 * tm)),
        ],
        out_specs=[
            pl.BlockSpec((1, N_KV_HEADS, tm // Q_TILE, HEAD_DIM, GQA_GROUP * Q_TILE), lambda b, i: (b, 0, i, 0, 0)),
            pl.BlockSpec((1, N_KV_HEADS, tm, HEAD_DIM), lambda b, i: (b, 0, i, 0)),
            pl.BlockSpec((1, N_KV_HEADS, 1, VT_ROWS, tm), lambda b, i: (b, 0, i, 0, 0)),
            pl.BlockSpec((1, tm, 2 * ATTN_WIDTH), lambda b, i: (b, i, 0)),
            pl.BlockSpec((1, 1, 2, tm, FOURIER_WIDTH), lambda b, i: (b, i, 0, 0, 0)),
        ],
        out_shape=[
            jax.ShapeDtypeStruct((B, N_KV_HEADS, S // Q_TILE, HEAD_DIM, GQA_GROUP * Q_TILE), BF16),
            jax.ShapeDtypeStruct((B, N_KV_HEADS, S, HEAD_DIM), BF16),
            jax.ShapeDtypeStruct((B, N_KV_HEADS, n_t, VT_ROWS, tm), BF16),
            jax.ShapeDtypeStruct((B, S, 2 * ATTN_WIDTH), BF16),
            jax.ShapeDtypeStruct((B, n_t, 2, tm, FOURIER_WIDTH), BF16),
        ],
        compiler_params=_params(("parallel", "parallel")),
        name="proj",
    )(x4, lnw, win, gq, gk, cos_t, sin_t, hmean, cs, m1)


def _attn_kernel(qT_ref, k_ref, vT_ref, o_ref, m_ref, acc_ref, s_ref, cmax_ref, *, n_chunks, tk, unroll):
    tq = o_ref.shape[1]
    qT = qT_ref[0, 0, 0]
    m_ref[...] = jnp.full(m_ref.shape, -jnp.inf, F32)
    acc_ref[...] = jnp.zeros(acc_ref.shape, F32)

    def scores(c, slot):
        off = pl.multiple_of(c * tk, tk)
        s = jnp.dot(k_ref[0, 0, pl.ds(off, tk), :], qT, preferred_element_type=F32)
        s_ref[slot] = s
        cmax_ref[slot] = jnp.max(s, axis=0, keepdims=True)

    def accumulate(c, slot):
        m_old = m_ref[...]
        m_new = jnp.maximum(m_old, cmax_ref[slot])
        p = jnp.exp2(s_ref[slot] - m_new).astype(BF16)
        alpha = jnp.exp2(m_old - m_new)
        pv = jnp.dot(vT_ref[0, 0, c], p, preferred_element_type=F32)
        acc_ref[...] = alpha * acc_ref[...] + pv
        m_ref[...] = m_new

    scores(0, 0)

    def group(i, carry):
        c = unroll * i
        for t in range(unroll):
            scores(c + t + 1, (t + 1) % 2)
            accumulate(c + t, t % 2)
        return carry

    lax.fori_loop(0, n_chunks // unroll - 1, group, 0)
    c = n_chunks - unroll
    for t in range(unroll):
        if t < unroll - 1:
            scores(c + t + 1, (t + 1) % 2)
        accumulate(c + t, t % 2)
    acc = acc_ref[...]
    oT = acc[:HEAD_DIM] / acc[HEAD_DIM:HEAD_DIM + 1]
    oT = jnp.concatenate([oT[:, hh * tq:(hh + 1) * tq] for hh in range(GQA_GROUP)], axis=0)
    o_ref[0] = oT.T.astype(BF16)


def _attn_call(qT, k, vT):
    B, _, n_q, _, _ = qT.shape
    S = k.shape[2]
    n_chunks, tk = vT.shape[2], vT.shape[4]
    unroll = 4 if n_chunks >= 16 else 2
    assert n_chunks % unroll == 0 and n_chunks >= 2 * unroll
    tq = Q_TILE
    width = GQA_GROUP * HEAD_DIM
    return pl.pallas_call(
        functools.partial(_attn_kernel, n_chunks=n_chunks, tk=tk, unroll=unroll),
        grid=(B, N_KV_HEADS, n_q),
        in_specs=[
            pl.BlockSpec((1, 1, 1, HEAD_DIM, GQA_GROUP * tq), lambda b, j, i: (b, j, i, 0, 0)),
            pl.BlockSpec((1, 1, S, HEAD_DIM), lambda b, j, i: (b, j, 0, 0)),
            pl.BlockSpec((1, 1, n_chunks, VT_ROWS, tk), lambda b, j, i: (b, j, 0, 0, 0)),
        ],
        out_specs=pl.BlockSpec((1, tq, width), lambda b, j, i: (b, i, j)),
        out_shape=jax.ShapeDtypeStruct((B, S, ATTN_WIDTH), BF16),
        scratch_shapes=[pltpu.VMEM((1, GQA_GROUP * tq), F32), pltpu.VMEM((VT_ROWS, GQA_GROUP * tq), F32),
                        pltpu.VMEM((2, tk, GQA_GROUP * tq), F32), pltpu.VMEM((2, 1, GQA_GROUP * tq), F32)],
        compiler_params=_params(("parallel", "parallel", "parallel")),
        name="attn",
    )(qT, k, vT)


def _out_kernel(x_ref, ya_ref, gate_ref, y_ref, g_ref, wf_ref, bf_ref, wout_ref, fn_ref, o_ref, *, scale):
    tm = ROW_TILE
    gate_a = gate_ref[0, :, :ATTN_WIDTH].astype(F32)
    gate_f = gate_ref[0, :, ATTN_WIDTH:].astype(F32)
    ya = (ya_ref[0].astype(F32) * gate_a).astype(BF16)
    y = y_ref[0].reshape(g_ref.shape[2], FOURIER_WIDTH)
    mix = (jnp.dot(g_ref[0], y, preferred_element_type=F32) * scale).astype(BF16)
    parts = []
    for g in range(N_FOURIER_GROUPS):
        lo = g * FOURIER_GROUP_DIM
        parts.append(jnp.dot(mix[:, lo:lo + FOURIER_GROUP_DIM], wf_ref[g], preferred_element_type=F32))
    yf = ((jnp.concatenate(parts, axis=-1) + bf_ref[...]) * gate_f).astype(BF16)
    out = x_ref[0].reshape(tm, D_MODEL)
    out = out + jnp.dot(ya, wout_ref[:ATTN_WIDTH, :], preferred_element_type=F32)
    out = out + jnp.dot(yf, wout_ref[ATTN_WIDTH:, :], preferred_element_type=F32)
    ms = jnp.mean(out * out, axis=-1, keepdims=True)
    o_ref[0] = (out * lax.rsqrt(ms + EPS) * fn_ref[...]).reshape(DFT_A, TILE_J, D_MODEL)


def _out_call(x4, ya, gates, y, g, wf, bf, wout, fnorm):
    B, _, S2, _ = x4.shape
    S = DFT_A * S2
    tm = ROW_TILE
    n_t = S // tm
    k1_rows = TILE_J * TILE_J
    n_k1_blocks = DFT_A // TILE_J
    const = lambda shape: pl.BlockSpec(shape, lambda i, b: (0,) * len(shape))
    scale = 1.0 / math.sqrt(S * FOURIER_GROUP_DIM)
    return pl.pallas_call(
        functools.partial(_out_kernel, scale=scale),
        grid=(n_t, B),
        in_specs=[
            pl.BlockSpec((1, DFT_A, TILE_J, D_MODEL), lambda i, b: (b, 0, i, 0)),
            pl.BlockSpec((1, tm, ATTN_WIDTH), lambda i, b: (b, i, 0)),
            pl.BlockSpec((1, tm, 2 * ATTN_WIDTH), lambda i, b: (b, i, 0)),
            pl.BlockSpec((1, n_t, 2, k1_rows, FOURIER_WIDTH), lambda i, b: (b, 0, 0, i % n_k1_blocks, 0)),
            pl.BlockSpec((1, tm, n_t * 2 * k1_rows), lambda i, b: (i, 0, 0)),
            const((N_FOURIER_GROUPS, FOURIER_GROUP_DIM, FOURIER_GROUP_DIM)),
            const((1, FOURIER_WIDTH)),
            const((D_MODEL, D_MODEL)),
            const((1, D_MODEL)),
        ],
        out_specs=pl.BlockSpec((1, DFT_A, TILE_J, D_MODEL), lambda i, b: (b, 0, i, 0)),
        out_shape=jax.ShapeDtypeStruct((B, DFT_A, S2, D_MODEL), F32),
        compiler_params=_params(("parallel", "parallel")),
        name="out_proj",
    )(x4, ya, gates, y, g, wf, bf, wout, fnorm)


def _tile_positions(seq_len):
    s2 = seq_len // DFT_A
    i = np.arange(s2 // TILE_J)[:, None, None]
    a = np.arange(DFT_A)[None, :, None]
    j = np.arange(TILE_J)[None, None, :]
    return (a * s2 + TILE_J * i + j).reshape(-1)


def _rope_tables(seq_len):
    pos = _tile_positions(seq_len)
    axis_dim = HEAD_DIM // 2
    inv_freq = ROPE_THETA ** (-np.arange(0, axis_dim, 2, dtype=np.float64) / axis_dim)
    row = (pos // GRID_W).astype(np.float64)
    col = (pos % GRID_W).astype(np.float64)
    ang = np.concatenate([row[:, None] * inv_freq, col[:, None] * inv_freq], axis=-1)
    cos = jnp.asarray(np.cos(ang).astype(np.float32))
    sin = jnp.asarray(np.sin(ang).astype(np.float32))
    cos_pair = jnp.repeat(cos, 2, axis=-1)
    sin_pair = jnp.stack([-sin, sin], axis=-1).reshape(seq_len, HEAD_DIM)
    return jnp.tile(cos_pair, (1, N_KV_HEADS)), jnp.tile(sin_pair, (1, N_KV_HEADS))


def _dft_cos_sin(n):
    idx = np.arange(n, dtype=np.int64)
    ang = ((idx[:, None] * idx[None, :]) % n).astype(np.float64) * (2.0 * math.pi / n)
    return np.cos(ang), np.sin(ang)


def _stage1_matrix():
    c, s = _dft_cos_sin(DFT_A)
    eye = np.eye(TILE_J)
    blocks = [[np.kron(c, eye), np.kron(-s, eye)], [np.kron(-s, eye), np.kron(-c, eye)]]
    return jnp.asarray(np.block(blocks).astype(np.float32)).astype(BF16)


def _expand_kernel(t_ref, r_ref, g_ref):
    g = jnp.dot(t_ref[0], r_ref[...], preferred_element_type=F32)
    row_j = lax.broadcasted_iota(jnp.int32, g.shape, 0) % TILE_J
    col_jj = (lax.broadcasted_iota(jnp.int32, g.shape, 1) // TILE_J) % TILE_J
    g_ref[0] = jnp.where(row_j == col_jj, g, 0.0).astype(BF16)


def _stage2_matrices(seq_len):
    s2_len = seq_len // DFT_A
    n_t = s2_len // TILE_J
    k = _tile_positions(seq_len).astype(np.int64)
    s2 = np.arange(s2_len, dtype=np.int64)
    ang = ((k[:, None] * s2[None, :]) % seq_len).astype(np.float64) * (2.0 * math.pi / seq_len)
    trig = np.stack([np.cos(ang), np.sin(ang)], axis=1)
    trig = trig.reshape(n_t, ROW_TILE, 2, n_t, TILE_J).transpose(0, 1, 3, 2, 4)
    n_in = n_t * 2 * TILE_J
    n_out = n_in * TILE_J
    compact = jnp.asarray(trig.reshape(n_t, ROW_TILE, n_in).astype(np.float32)).astype(BF16)
    copy_j = np.kron(np.ones((1, TILE_J)), np.eye(TILE_J))
    replicate = jnp.asarray(np.kron(np.eye(n_t * 2), copy_j), BF16)
    return pl.pallas_call(
        _expand_kernel,
        grid=(n_t,),
        in_specs=[
            pl.BlockSpec((1, ROW_TILE, n_in), lambda i: (i, 0, 0)),
            pl.BlockSpec((n_in, n_out), lambda i: (0, 0)),
        ],
        out_specs=pl.BlockSpec((1, ROW_TILE, n_out), lambda i: (i, 0, 0)),
        out_shape=jax.ShapeDtypeStruct((n_t, ROW_TILE, n_out), BF16),
        compiler_params=_params(("parallel",)),
        name="dft_stage2_matrix",
    )(compact, replicate)


def _trunk(x, lnw, win, gq, gk, hmean, cs, m1, wf, bf, wout, fnorm):
    B, S, _ = x.shape
    assert S % ROW_TILE == 0
    s2 = S // DFT_A
    x4 = x.reshape(B, DFT_A, s2, D_MODEL)
    cos_t, sin_t = _rope_tables(S)
    qT, k, vT, gates, y = _proj_call(x4, lnw, win, gq, gk, cos_t, sin_t, hmean, cs, m1)
    ya = _attn_call(qT, k, vT)
    g = _stage2_matrices(S)
    out = _out_call(x4, ya, gates, y, g, wf, bf, wout, fnorm)
    return out.reshape(B, S, D_MODEL)


def kernel(x_prompt, x_sample, ln_w, w_in, q_norm, k_norm, w_fourier, b_fourier, w_out, final_norm):
    assert ln_w.shape[0] == 1, "single mixer layer"
    lnw = ln_w[0].reshape(1, D_MODEL)
    win = w_in[0].astype(BF16)
    q_scale = HEAD_DIM ** -0.5 * math.log2(math.e)
    gq = jnp.tile(q_norm[0] * q_scale, N_Q_HEADS).reshape(1, ATTN_WIDTH)
    gk = jnp.tile(k_norm[0], N_KV_HEADS).reshape(1, KV_WIDTH)
    head_id = np.arange(ATTN_WIDTH) // HEAD_DIM
    hmean = jnp.asarray((head_id[:, None] == head_id[None, :]) / HEAD_DIM, BF16)
    c_c, s_c = _dft_cos_sin(FOURIER_GROUP_DIM)
    cs = jnp.asarray(np.concatenate([c_c, s_c], axis=1).astype(np.float32)).astype(BF16)
    m1 = _stage1_matrix()
    wf = w_fourier[0].astype(BF16)
    bf = b_fourier[0].reshape(1, FOURIER_WIDTH)
    wout = w_out[0].astype(BF16)
    fnorm = final_norm.reshape(1, D_MODEL)
    args = (lnw, win, gq, gk, hmean, cs, m1, wf, bf, wout, fnorm)
    return (_trunk(x_prompt, *args), _trunk(x_sample, *args))
```

```python
import functools
import math

import jax
import jax.numpy as jnp
import numpy as np
from jax import lax
from jax.experimental import pallas as pl
from jax.experimental.pallas import tpu as pltpu

D_MODEL = 1024
GRID_W = 64
ATTN_WIDTH = 512
FOURIER_WIDTH = 512
HEAD_DIM = 64
N_Q_HEADS = 8
N_KV_HEADS = 2
GQA_GROUP = 4
KV_WIDTH = 128
N_FOURIER_GROUPS = 4
FOURIER_GROUP_DIM = 128
ROPE_THETA = 10000.0
EPS = 1e-6
IN_WIDTH = 2304
Q0, K0, V0, GA0, U0, GF0 = 0, 512, 640, 768, 1280, 1792

VT_ROWS = HEAD_DIM + 16

DFT_A = 64
TILE_J = 8
ROW_TILE = DFT_A * TILE_J
Q_TILE = 256
VMEM_LIMIT_BYTES = 48 * 1024 * 1024

F32 = jnp.float32
BF16 = jnp.bfloat16


def _params(semantics):
    return pltpu.CompilerParams(dimension_semantics=semantics, vmem_limit_bytes=VMEM_LIMIT_BYTES)


def _rope(x, cos, sin_signed):
    n = x.shape[-1]
    lane_is_even = (lax.broadcasted_iota(jnp.int32, x.shape, 1) & 1) == 0
    partner = jnp.where(lane_is_even, pltpu.roll(x, n - 1, 1), pltpu.roll(x, 1, 1))
    return x * cos + partner * sin_signed


def _proj_kernel(x_ref, lnw_ref, win_ref, gq_ref, gk_ref, cos_ref, sin_ref, hmean_ref, cs_ref, m1_ref,
                 qT_ref, k_ref, vT_ref, gate_ref, y_ref):
    tm = ROW_TILE
    x = x_ref[0].reshape(tm, D_MODEL)
    ms = jnp.mean(x * x, axis=-1, keepdims=True)
    h = (x * lax.rsqrt(ms + EPS) * lnw_ref[...]).astype(BF16)
    proj = jnp.dot(h, win_ref[...], preferred_element_type=F32)

    cos2 = cos_ref[...]
    sin2 = sin_ref[...]
    cos = jnp.concatenate([cos2] * (ATTN_WIDTH // KV_WIDTH), axis=1)
    sin = jnp.concatenate([sin2] * (ATTN_WIDTH // KV_WIDTH), axis=1)
    hmean = hmean_ref[...]

    q = proj[:, Q0:Q0 + ATTN_WIDTH]
    q_ms = jnp.dot((q * q).astype(BF16), hmean, preferred_element_type=F32)
    q = q * lax.rsqrt(q_ms + EPS) * gq_ref[...]
    q = _rope(q, cos, sin)
    qT = q.T.astype(BF16)
    for hd in range(N_Q_HEADS):
        j, hh = divmod(hd, GQA_GROUP)
        for t in range(tm // Q_TILE):
            qT_ref[0, j, t, :, hh * Q_TILE:(hh + 1) * Q_TILE] = (
                qT[hd * HEAD_DIM:(hd + 1) * HEAD_DIM, t * Q_TILE:(t + 1) * Q_TILE])

    k = proj[:, K0:K0 + KV_WIDTH]
    k_ms = jnp.dot((k * k).astype(BF16), hmean[:KV_WIDTH, :KV_WIDTH], preferred_element_type=F32)
    k = k * lax.rsqrt(k_ms + EPS) * gk_ref[...]
    k = _rope(k, cos2, sin2).astype(BF16)
    for j in range(N_KV_HEADS):
        k_ref[0, j] = k[:, j * HEAD_DIM:(j + 1) * HEAD_DIM]

    vT = proj[:, V0:V0 + KV_WIDTH].T.astype(BF16)
    ones = jnp.ones((VT_ROWS - HEAD_DIM, tm), BF16)
    for j in range(N_KV_HEADS):
        vT_ref[0, j, 0, :HEAD_DIM, :] = vT[j * HEAD_DIM:(j + 1) * HEAD_DIM]
        vT_ref[0, j, 0, HEAD_DIM:, :] = ones

    gate_ref[0, :, :ATTN_WIDTH] = jax.nn.silu(proj[:, GA0:GA0 + ATTN_WIDTH]).astype(BF16)
    gate_ref[0, :, ATTN_WIDTH:] = jax.nn.silu(proj[:, GF0:GF0 + FOURIER_WIDTH]).astype(BF16)

    cs = cs_ref[...]
    a_parts, b_parts = [], []
    for g in range(N_FOURIER_GROUPS):
        lo = g * FOURIER_GROUP_DIM
        u = proj[:, U0 + lo:U0 + lo + FOURIER_GROUP_DIM].astype(BF16)
        ab = jnp.dot(u, cs, preferred_element_type=F32)
        a_parts.append(ab[:, :FOURIER_GROUP_DIM].astype(BF16))
        b_parts.append(ab[:, FOURIER_GROUP_DIM:].astype(BF16))
    z = jnp.concatenate([jnp.concatenate(a_parts, axis=1), jnp.concatenate(b_parts, axis=1)], axis=0)
    y = jnp.dot(m1_ref[...], z, preferred_element_type=F32).astype(BF16)
    y_ref[0, 0, 0] = y[:tm]
    y_ref[0, 0, 1] = y[tm:]


def _proj_call(x4, lnw, win, gq, gk, cos_t, sin_t, hmean, cs, m1):
    B, _, S2, _ = x4.shape
    S = DFT_A * S2
    tm = ROW_TILE
    n_t = S // tm
    const = lambda shape: pl.BlockSpec(shape, lambda b, i: (0,) * len(shape))
    return pl.pallas_call(
        _proj_kernel,
        grid=(B, n_t),
        in_specs=[
            pl.BlockSpec((1, DFT_A, TILE_J, D_MODEL), lambda b, i: (b, 0, i, 0)),
            const((1, D_MODEL)),
            const((D_MODEL, IN_WIDTH)),
            const((1, ATTN_WIDTH)),
            const((1, KV_WIDTH)),
            pl.BlockSpec((tm, KV_WIDTH), lambda b, i: (i, 0)),
            pl.BlockSpec((tm, KV_WIDTH), lambda b, i: (i, 0)),
            const((ATTN_WIDTH, ATTN_WIDTH)),
            const((FOURIER_GROUP_DIM, 2 * FOURIER_GROUP_DIM)),
            const((2 * tm, 2 * tm)),
        ],
        out_specs=[
            pl.BlockSpec((1, N_KV_HEADS, tm // Q_TILE, HEAD_DIM, GQA_GROUP * Q_TILE), lambda b, i: (b, 0, i, 0, 0)),
            pl.BlockSpec((1, N_KV_HEADS, tm, HEAD_DIM), lambda b, i: (b, 0, i, 0)),
            pl.BlockSpec((1, N_KV_HEADS, 1, VT_ROWS, tm), lambda b, i: (b, 0, i, 0, 0)),
            pl.BlockSpec((1, tm, 2 * ATTN_WIDTH), lambda b, i: (b, i, 0)),
            pl.BlockSpec((1, 1, 2, tm, FOURIER_WIDTH), lambda b, i: (b, i, 0, 0, 0)),
        ],
        out_shape=[
            jax.ShapeDtypeStruct((B, N_KV_HEADS, S // Q_TILE, HEAD_DIM, GQA_GROUP * Q_TILE), BF16),
            jax.ShapeDtypeStruct((B, N_KV_HEADS, S, HEAD_DIM), BF16),
            jax.ShapeDtypeStruct((B, N_KV_HEADS, n_t, VT_ROWS, tm), BF16),
            jax.ShapeDtypeStruct((B, S, 2 * ATTN_WIDTH), BF16),
            jax.ShapeDtypeStruct((B, n_t, 2, tm, FOURIER_WIDTH), BF16),
        ],
        compiler_params=_params(("parallel", "parallel")),
        name="proj",
    )(x4, lnw, win, gq, gk, cos_t, sin_t, hmean, cs, m1)


def _attn_kernel(qT_ref, k_ref, vT_ref, o_ref, m_ref, acc_ref, s_ref, cmax_ref, *, n_chunks, tk, unroll):
    n_q = qT_ref.shape[2]
    tq = qT_ref.shape[4] // GQA_GROUP

    def scores(t, c, slot):
        off = pl.multiple_of(c * tk, tk)
        s = jnp.dot(k_ref[0, 0, pl.ds(off, tk), :], qT_ref[0, 0, t], preferred_element_type=F32)
        s_ref[slot] = s
        cmax_ref[slot] = jnp.max(s, axis=0, keepdims=True)

    def accumulate(c, slot):
        m_old = m_ref[...]
        m_new = jnp.maximum(m_old, cmax_ref[slot])
        p = jnp.exp2(s_ref[slot] - m_new).astype(BF16)
        alpha = jnp.exp2(m_old - m_new)
        pv = jnp.dot(vT_ref[0, 0, c], p, preferred_element_type=F32)
        acc_ref[...] = alpha * acc_ref[...] + pv
        m_ref[...] = m_new

    scores(0, 0, 0)

    def tile(t, carry):
        m_ref[...] = jnp.full(m_ref.shape, -jnp.inf, F32)
        acc_ref[...] = jnp.zeros(acc_ref.shape, F32)

        def group(i, inner):
            c = unroll * i
            for u in range(unroll):
                scores(t, c + u + 1, (u + 1) % 2)
                accumulate(c + u, u % 2)
            return inner

        lax.fori_loop(0, n_chunks // unroll - 1, group, 0)
        c = n_chunks - unroll
        for u in range(unroll):
            if u < unroll - 1:
                scores(t, c + u + 1, (u + 1) % 2)
            else:
                scores(jnp.minimum(t + 1, n_q - 1), 0, 0)
            accumulate(c + u, u % 2)
        acc = acc_ref[...]
        oT = acc[:HEAD_DIM] / acc[HEAD_DIM:HEAD_DIM + 1]
        oT = jnp.concatenate([oT[:, hh * tq:(hh + 1) * tq] for hh in range(GQA_GROUP)], axis=0)
        o_ref[0, pl.ds(pl.multiple_of(t * tq, tq), tq), :] = oT.T.astype(BF16)
        return carry

    lax.fori_loop(0, n_q, tile, 0)


def _attn_call(qT, k, vT):
    B, _, n_q, _, _ = qT.shape
    S = k.shape[2]
    n_chunks, tk = vT.shape[2], vT.shape[4]
    unroll = 4 if n_chunks >= 16 else 2
    assert n_chunks % unroll == 0 and n_chunks >= 2 * unroll
    tq = Q_TILE
    width = GQA_GROUP * HEAD_DIM
    return pl.pallas_call(
        functools.partial(_attn_kernel, n_chunks=n_chunks, tk=tk, unroll=unroll),
        grid=(B, N_KV_HEADS),
        in_specs=[
            pl.BlockSpec((1, 1, n_q, HEAD_DIM, GQA_GROUP * tq), lambda b, j: (b, j, 0, 0, 0)),
            pl.BlockSpec((1, 1, S, HEAD_DIM), lambda b, j: (b, j, 0, 0)),
            pl.BlockSpec((1, 1, n_chunks, VT_ROWS, tk), lambda b, j: (b, j, 0, 0, 0)),
        ],
        out_specs=pl.BlockSpec((1, S, width), lambda b, j: (b, 0, j)),
        out_shape=jax.ShapeDtypeStruct((B, S, ATTN_WIDTH), BF16),
        scratch_shapes=[pltpu.VMEM((1, GQA_GROUP * tq), F32), pltpu.VMEM((VT_ROWS, GQA_GROUP * tq), F32),
                        pltpu.VMEM((2, tk, GQA_GROUP * tq), F32), pltpu.VMEM((2, 1, GQA_GROUP * tq), F32)],
        compiler_params=_params(("parallel", "parallel")),
        name="attn",
    )(qT, k, vT)


def _out_kernel(x_ref, ya_ref, gate_ref, y_ref, g_ref, wf_ref, bf_ref, wout_ref, fn_ref, o_ref, *, scale):
    tm = ROW_TILE
    gate_a = gate_ref[0, :, :ATTN_WIDTH].astype(F32)
    gate_f = gate_ref[0, :, ATTN_WIDTH:].astype(F32)
    ya = (ya_ref[0].astype(F32) * gate_a).astype(BF16)
    y = y_ref[0].reshape(g_ref.shape[2], FOURIER_WIDTH)
    mix = (jnp.dot(g_ref[0], y, preferred_element_type=F32) * scale).astype(BF16)
    parts = []
    for g in range(N_FOURIER_GROUPS):
        lo = g * FOURIER_GROUP_DIM
        parts.append(jnp.dot(mix[:, lo:lo + FOURIER_GROUP_DIM], wf_ref[g], preferred_element_type=F32))
    yf = ((jnp.concatenate(parts, axis=-1) + bf_ref[...]) * gate_f).astype(BF16)
    out = x_ref[0].reshape(tm, D_MODEL)
    out = out + jnp.dot(ya, wout_ref[:ATTN_WIDTH, :], preferred_element_type=F32)
    out = out + jnp.dot(yf, wout_ref[ATTN_WIDTH:, :], preferred_element_type=F32)
    ms = jnp.mean(out * out, axis=-1, keepdims=True)
    o_ref[0] = (out * lax.rsqrt(ms + EPS) * fn_ref[...]).reshape(DFT_A, TILE_J, D_MODEL)


def _out_call(x4, ya, gates, y, g, wf, bf, wout, fnorm):
    B, _, S2, _ = x4.shape
    S = DFT_A * S2
    tm = ROW_TILE
    n_t = S // tm
    k1_rows = TILE_J * TILE_J
    n_k1_blocks = DFT_A // TILE_J
    const = lambda shape: pl.BlockSpec(shape, lambda i, b: (0,) * len(shape))
    scale = 1.0 / math.sqrt(S * FOURIER_GROUP_DIM)
    return pl.pallas_call(
        functools.partial(_out_kernel, scale=scale),
        grid=(n_t, B),
        in_specs=[
            pl.BlockSpec((1, DFT_A, TILE_J, D_MODEL), lambda i, b: (b, 0, i, 0)),
            pl.BlockSpec((1, tm, ATTN_WIDTH), lambda i, b: (b, i, 0)),
            pl.BlockSpec((1, tm, 2 * ATTN_WIDTH), lambda i, b: (b, i, 0)),
            pl.BlockSpec((1, n_t, 2, k1_rows, FOURIER_WIDTH), lambda i, b: (b, 0, 0, i % n_k1_blocks, 0)),
            pl.BlockSpec((1, tm, n_t * 2 * k1_rows), lambda i, b: (i, 0, 0)),
            const((N_FOURIER_GROUPS, FOURIER_GROUP_DIM, FOURIER_GROUP_DIM)),
            const((1, FOURIER_WIDTH)),
            const((D_MODEL, D_MODEL)),
            const((1, D_MODEL)),
        ],
        out_specs=pl.BlockSpec((1, DFT_A, TILE_J, D_MODEL), lambda i, b: (b, 0, i, 0)),
        out_shape=jax.ShapeDtypeStruct((B, DFT_A, S2, D_MODEL), F32),
        compiler_params=_params(("parallel", "parallel")),
        name="out_proj",
    )(x4, ya, gates, y, g, wf, bf, wout, fnorm)


def _tile_positions(seq_len):
    s2 = seq_len // DFT_A
    i = np.arange(s2 // TILE_J)[:, None, None]
    a = np.arange(DFT_A)[None, :, None]
    j = np.arange(TILE_J)[None, None, :]
    return (a * s2 + TILE_J * i + j).reshape(-1)


def _rope_tables(seq_len):
    pos = _tile_positions(seq_len)
    axis_dim = HEAD_DIM // 2
    inv_freq = ROPE_THETA ** (-np.arange(0, axis_dim, 2, dtype=np.float64) / axis_dim)
    row = (pos // GRID_W).astype(np.float64)
    col = (pos % GRID_W).astype(np.float64)
    ang = np.concatenate([row[:, None] * inv_freq, col[:, None] * inv_freq], axis=-1)
    cos = jnp.asarray(np.cos(ang).astype(np.float32))
    sin = jnp.asarray(np.sin(ang).astype(np.float32))
    cos_pair = jnp.repeat(cos, 2, axis=-1)
    sin_pair = jnp.stack([-sin, sin], axis=-1).reshape(seq_len, HEAD_DIM)
    return jnp.tile(cos_pair, (1, N_KV_HEADS)), jnp.tile(sin_pair, (1, N_KV_HEADS))


def _dft_cos_sin(n):
    idx = np.arange(n, dtype=np.int64)
    ang = ((idx[:, None] * idx[None, :]) % n).astype(np.float64) * (2.0 * math.pi / n)
    return np.cos(ang), np.sin(ang)


def _stage1_matrix():
    c, s = _dft_cos_sin(DFT_A)
    eye = np.eye(TILE_J)
    blocks = [[np.kron(c, eye), np.kron(-s, eye)], [np.kron(-s, eye), np.kron(-c, eye)]]
    return jnp.asarray(np.block(blocks).astype(np.float32)).astype(BF16)


def _expand_kernel(t_ref, r_ref, g_ref):
    g = jnp.dot(t_ref[0], r_ref[...], preferred_element_type=F32)
    row_j = lax.broadcasted_iota(jnp.int32, g.shape, 0) % TILE_J
    col_jj = (lax.broadcasted_iota(jnp.int32, g.shape, 1) // TILE_J) % TILE_J
    g_ref[0] = jnp.where(row_j == col_jj, g, 0.0).astype(BF16)


def _stage2_matrices(seq_len):
    s2_len = seq_len // DFT_A
    n_t = s2_len // TILE_J
    k = _tile_positions(seq_len).astype(np.int64)
    s2 = np.arange(s2_len, dtype=np.int64)
    ang = ((k[:, None] * s2[None, :]) % seq_len).astype(np.float64) * (2.0 * math.pi / seq_len)
    trig = np.stack([np.cos(ang), np.sin(ang)], axis=1)
    trig = trig.reshape(n_t, ROW_TILE, 2, n_t, TILE_J).transpose(0, 1, 3, 2, 4)
    n_in = n_t * 2 * TILE_J
    n_out = n_in * TILE_J
    compact = jnp.asarray(trig.reshape(n_t, ROW_TILE, n_in).astype(np.float32)).astype(BF16)
    copy_j = np.kron(np.ones((1, TILE_J)), np.eye(TILE_J))
    replicate = jnp.asarray(np.kron(np.eye(n_t * 2), copy_j), BF16)
    return pl.pallas_call(
        _expand_kernel,
        grid=(n_t,),
        in_specs=[
            pl.BlockSpec((1, ROW_TILE, n_in), lambda i: (i, 0, 0)),
            pl.BlockSpec((n_in, n_out), lambda i: (0, 0)),
        ],
        out_specs=pl.BlockSpec((1, ROW_TILE, n_out), lambda i: (i, 0, 0)),
        out_shape=jax.ShapeDtypeStruct((n_t, ROW_TILE, n_out), BF16),
        compiler_params=_params(("parallel",)),
        name="dft_stage2_matrix",
    )(compact, replicate)


def _trunk(x, lnw, win, gq, gk, hmean, cs, m1, wf, bf, wout, fnorm):
    B, S, _ = x.shape
    assert S % ROW_TILE == 0
    s2 = S // DFT_A
    x4 = x.reshape(B, DFT_A, s2, D_MODEL)
    cos_t, sin_t = _rope_tables(S)
    qT, k, vT, gates, y = _proj_call(x4, lnw, win, gq, gk, cos_t, sin_t, hmean, cs, m1)
    ya = _attn_call(qT, k, vT)
    g = _stage2_matrices(S)
    out = _out_call(x4, ya, gates, y, g, wf, bf, wout, fnorm)
    return out.reshape(B, S, D_MODEL)


def kernel(x_prompt, x_sample, ln_w, w_in, q_norm, k_norm, w_fourier, b_fourier, w_out, final_norm):
    assert ln_w.shape[0] == 1, "single mixer layer"
    lnw = ln_w[0].reshape(1, D_MODEL)
    win = w_in[0].astype(BF16)
    q_scale = HEAD_DIM ** -0.5 * math.log2(math.e)
    gq = jnp.tile(q_norm[0] * q_scale, N_Q_HEADS).reshape(1, ATTN_WIDTH)
    gk = jnp.tile(k_norm[0], N_KV_HEADS).reshape(1, KV_WIDTH)
    head_id = np.arange(ATTN_WIDTH) // HEAD_DIM
    hmean = jnp.asarray((head_id[:, None] == head_id[None, :]) / HEAD_DIM, BF16)
    c_c, s_c = _dft_cos_sin(FOURIER_GROUP_DIM)
    cs = jnp.asarray(np.concatenate([c_c, s_c], axis=1).astype(np.float32)).astype(BF16)
    m1 = _stage1_matrix()
    wf = w_fourier[0].astype(BF16)
    bf = b_fourier[0].reshape(1, FOURIER_WIDTH)
    wout = w_out[0].astype(BF16)
    fnorm = final_norm.reshape(1, D_MODEL)
    args = (lnw, win, gq, gk, hmean, cs, m1, wf, bf, wout, fnorm)
    return (_trunk(x_prompt, *args), _trunk(x_sample, *args))
```

```python
import functools
import math

import jax
import jax.numpy as jnp
import numpy as np
from jax import lax
from jax.experimental import pallas as pl
from jax.experimental.pallas import tpu as pltpu

D_MODEL = 1024
GRID_W = 64
ATTN_WIDTH = 512
FOURIER_WIDTH = 512
HEAD_DIM = 64
N_Q_HEADS = 8
N_KV_HEADS = 2
GQA_GROUP = 4
KV_WIDTH = 128
N_FOURIER_GROUPS = 4
FOURIER_GROUP_DIM = 128
ROPE_THETA = 10000.0
EPS = 1e-6
IN_WIDTH = 2304
Q0, K0, V0, GA0, U0, GF0 = 0, 512, 640, 768, 1280, 1792

VT_ROWS = HEAD_DIM + 16

DFT_A = 64
SUBLANES = 8
TILE_J = SUBLANES
ROW_TILE = DFT_A * TILE_J
Q_TILE = 256
VMEM_LIMIT_BYTES = 48 * 1024 * 1024

F32 = jnp.float32
BF16 = jnp.bfloat16


def _params(semantics):
    return pltpu.CompilerParams(dimension_semantics=semantics, vmem_limit_bytes=VMEM_LIMIT_BYTES)


def _rope(x, cos, sin_signed):
    n = x.shape[-1]
    lane_is_even = (lax.broadcasted_iota(jnp.int32, x.shape, 1) & 1) == 0
    partner = jnp.where(lane_is_even, pltpu.roll(x, n - 1, 1), pltpu.roll(x, 1, 1))
    return x * cos + partner * sin_signed


def _proj_kernel(x_ref, lnw_ref, win_ref, gq_ref, gk_ref, cos_ref, sin_ref, hmean_ref, cs_ref, m1_ref,
                 qT_ref, k_ref, vT_ref, gate_ref, y_ref):
    tm = ROW_TILE
    x = x_ref[0].reshape(tm, D_MODEL)
    ms = jnp.mean(x * x, axis=-1, keepdims=True)
    h = (x * lax.rsqrt(ms + EPS) * lnw_ref[...]).astype(BF16)
    proj = jnp.dot(h, win_ref[...], preferred_element_type=F32)

    cos2 = cos_ref[...]
    sin2 = sin_ref[...]
    cos = jnp.concatenate([cos2] * (ATTN_WIDTH // KV_WIDTH), axis=1)
    sin = jnp.concatenate([sin2] * (ATTN_WIDTH // KV_WIDTH), axis=1)
    hmean = hmean_ref[...]

    q = proj[:, Q0:Q0 + ATTN_WIDTH]
    q_ms = jnp.dot((q * q).astype(BF16), hmean, preferred_element_type=F32)
    q = q * lax.rsqrt(q_ms + EPS) * gq_ref[...]
    q = _rope(q, cos, sin)
    qT = q.T.astype(BF16)
    for hd in range(N_Q_HEADS):
        j, hh = divmod(hd, GQA_GROUP)
        for t in range(tm // Q_TILE):
            qT_ref[0, j, t, :, hh * Q_TILE:(hh + 1) * Q_TILE] = (
                qT[hd * HEAD_DIM:(hd + 1) * HEAD_DIM, t * Q_TILE:(t + 1) * Q_TILE])

    k = proj[:, K0:K0 + KV_WIDTH]
    k_ms = jnp.dot((k * k).astype(BF16), hmean[:KV_WIDTH, :KV_WIDTH], preferred_element_type=F32)
    k = k * lax.rsqrt(k_ms + EPS) * gk_ref[...]
    k = _rope(k, cos2, sin2).astype(BF16)
    for j in range(N_KV_HEADS):
        k_ref[0, j] = k[:, j * HEAD_DIM:(j + 1) * HEAD_DIM]

    vT = proj[:, V0:V0 + KV_WIDTH].T.astype(BF16)
    ones = jnp.ones((VT_ROWS - HEAD_DIM, tm), BF16)
    for j in range(N_KV_HEADS):
        vT_ref[0, j, 0, :HEAD_DIM, :] = vT[j * HEAD_DIM:(j + 1) * HEAD_DIM]
        vT_ref[0, j, 0, HEAD_DIM:, :] = ones

    gate_ref[0, :, :ATTN_WIDTH] = jax.nn.silu(proj[:, GA0:GA0 + ATTN_WIDTH]).astype(BF16)
    gate_ref[0, :, ATTN_WIDTH:] = jax.nn.silu(proj[:, GF0:GF0 + FOURIER_WIDTH]).astype(BF16)

    cs = cs_ref[...]
    a_parts, b_parts = [], []
    for g in range(N_FOURIER_GROUPS):
        lo = g * FOURIER_GROUP_DIM
        u = proj[:, U0 + lo:U0 + lo + FOURIER_GROUP_DIM].astype(BF16)
        ab = jnp.dot(u, cs, preferred_element_type=F32)
        a_parts.append(ab[:, :FOURIER_GROUP_DIM].astype(BF16))
        b_parts.append(ab[:, FOURIER_GROUP_DIM:].astype(BF16))
    z = jnp.concatenate([jnp.concatenate(a_parts, axis=1), jnp.concatenate(b_parts, axis=1)], axis=0)
    y = jnp.dot(m1_ref[...], z, preferred_element_type=F32).astype(BF16)
    y_ref[0, 0, 0] = y[:tm]
    y_ref[0, 0, 1] = y[tm:]


def _proj_call(x4, lnw, win, gq, gk, cos_t, sin_t, hmean, cs, m1):
    B, _, S2, _ = x4.shape
    S = DFT_A * S2
    tm = ROW_TILE
    n_t = S // tm
    const = lambda shape: pl.BlockSpec(shape, lambda b, i: (0,) * len(shape))
    return pl.pallas_call(
        _proj_kernel,
        grid=(B, n_t),
        in_specs=[
            pl.BlockSpec((1, DFT_A, TILE_J, D_MODEL), lambda b, i: (b, 0, i, 0)),
            const((1, D_MODEL)),
            const((D_MODEL, IN_WIDTH)),
            const((1, ATTN_WIDTH)),
            const((1, KV_WIDTH)),
            pl.BlockSpec((tm, KV_WIDTH), lambda b, i: (i, 0)),
            pl.BlockSpec((tm, KV_WIDTH), lambda b, i: (i, 0)),
            const((ATTN_WIDTH, ATTN_WIDTH)),
            const((FOURIER_GROUP_DIM, 2 * FOURIER_GROUP_DIM)),
            const((2 * tm, 2 * tm)),
        ],
        out_specs=[
            pl.BlockSpec((1, N_KV_HEADS, tm // Q_TILE, HEAD_DIM, GQA_GROUP * Q_TILE), lambda b, i: (b, 0, i, 0, 0)),
            pl.BlockSpec((1, N_KV_HEADS, tm, HEAD_DIM), lambda b, i: (b, 0, i, 0)),
            pl.BlockSpec((1, N_KV_HEADS, 1, VT_ROWS, tm), lambda b, i: (b, 0, i, 0, 0)),
            pl.BlockSpec((1, tm, 2 * ATTN_WIDTH), lambda b, i: (b, i, 0)),
            pl.BlockSpec((1, 1, 2, tm, FOURIER_WIDTH), lambda b, i: (b, i, 0, 0, 0)),
        ],
        out_shape=[
            jax.ShapeDtypeStruct((B, N_KV_HEADS, S // Q_TILE, HEAD_DIM, GQA_GROUP * Q_TILE), BF16),
            jax.ShapeDtypeStruct((B, N_KV_HEADS, S, HEAD_DIM), BF16),
            jax.ShapeDtypeStruct((B, N_KV_HEADS, n_t, VT_ROWS, tm), BF16),
            jax.ShapeDtypeStruct((B, S, 2 * ATTN_WIDTH), BF16),
            jax.ShapeDtypeStruct((B, n_t, 2, tm, FOURIER_WIDTH), BF16),
        ],
        compiler_params=_params(("parallel", "parallel")),
        name="proj",
    )(x4, lnw, win, gq, gk, cos_t, sin_t, hmean, cs, m1)


def _attn_kernel(qT_ref, k_ref, vT_ref, o_ref, m_ref, acc_ref, s_ref, cmax_ref, *, n_chunks, tk, unroll):
    n_q = qT_ref.shape[2]
    tq = qT_ref.shape[4] // GQA_GROUP

    def scores(t, c, slot):
        off = pl.multiple_of(c * tk, tk)
        s = jnp.dot(k_ref[0, 0, pl.ds(off, tk), :], qT_ref[0, 0, t], preferred_element_type=F32)
        s_ref[slot] = s
        cmax_ref[slot, 0:1] = jnp.max(s, axis=0, keepdims=True)

    def accumulate(c, slot):
        m_old = m_ref[0:1]
        m_new = jnp.maximum(m_old, cmax_ref[slot, 0:1])
        p = jnp.exp2(s_ref[slot] - m_new).astype(BF16)
        alpha = jnp.exp2(m_old - m_new)
        pv = jnp.dot(vT_ref[0, 0, c], p, preferred_element_type=F32)
        acc_ref[...] = alpha * acc_ref[...] + pv
        m_ref[0:1] = m_new

    scores(0, 0, 0)

    def tile(t, carry):
        m_ref[0:1] = jnp.full((1, m_ref.shape[1]), -jnp.inf, F32)
        acc_ref[...] = jnp.zeros(acc_ref.shape, F32)

        def group(i, inner):
            c = unroll * i
            for u in range(unroll):
                scores(t, c + u + 1, (u + 1) % 2)
                accumulate(c + u, u % 2)
            return inner

        lax.fori_loop(0, n_chunks // unroll - 1, group, 0)
        c = n_chunks - unroll
        for u in range(unroll):
            if u < unroll - 1:
                scores(t, c + u + 1, (u + 1) % 2)
            else:
                scores(jnp.minimum(t + 1, n_q - 1), 0, 0)
            accumulate(c + u, u % 2)
        acc = acc_ref[...]
        oT = acc[:HEAD_DIM] / acc[HEAD_DIM:HEAD_DIM + 1]
        oT = jnp.concatenate([oT[:, hh * tq:(hh + 1) * tq] for hh in range(GQA_GROUP)], axis=0)
        o_ref[0, pl.ds(pl.multiple_of(t * tq, tq), tq), :] = oT.T.astype(BF16)
        return carry

    lax.fori_loop(0, n_q, tile, 0)


def _attn_call(qT, k, vT):
    B, _, n_q, _, _ = qT.shape
    S = k.shape[2]
    n_chunks, tk = vT.shape[2], vT.shape[4]
    unroll = 4 if n_chunks >= 16 else 2
    assert n_chunks % unroll == 0 and n_chunks >= 2 * unroll
    tq = Q_TILE
    width = GQA_GROUP * HEAD_DIM
    return pl.pallas_call(
        functools.partial(_attn_kernel, n_chunks=n_chunks, tk=tk, unroll=unroll),
        grid=(B, N_KV_HEADS),
        in_specs=[
            pl.BlockSpec((1, 1, n_q, HEAD_DIM, GQA_GROUP * tq), lambda b, j: (b, j, 0, 0, 0)),
            pl.BlockSpec((1, 1, S, HEAD_DIM), lambda b, j: (b, j, 0, 0)),
            pl.BlockSpec((1, 1, n_chunks, VT_ROWS, tk), lambda b, j: (b, j, 0, 0, 0)),
        ],
        out_specs=pl.BlockSpec((1, S, width), lambda b, j: (b, 0, j)),
        out_shape=jax.ShapeDtypeStruct((B, S, ATTN_WIDTH), BF16),
        scratch_shapes=[pltpu.VMEM((SUBLANES, GQA_GROUP * tq), F32), pltpu.VMEM((VT_ROWS, GQA_GROUP * tq), F32),
                        pltpu.VMEM((2, tk, GQA_GROUP * tq), F32),
                        pltpu.VMEM((2, SUBLANES, GQA_GROUP * tq), F32)],
        compiler_params=_params(("parallel", "parallel")),
        name="attn",
    )(qT, k, vT)


def _out_kernel(x_ref, ya_ref, gate_ref, y_ref, g_ref, wf_ref, bf_ref, wout_ref, fn_ref, o_ref, *, scale):
    tm = ROW_TILE
    gate_a = gate_ref[0, :, :ATTN_WIDTH].astype(F32)
    gate_f = gate_ref[0, :, ATTN_WIDTH:].astype(F32)
    ya = (ya_ref[0].astype(F32) * gate_a).astype(BF16)
    y = y_ref[0].reshape(g_ref.shape[2], FOURIER_WIDTH)
    mix = (jnp.dot(g_ref[0], y, preferred_element_type=F32) * scale).astype(BF16)
    parts = []
    for g in range(N_FOURIER_GROUPS):
        lo = g * FOURIER_GROUP_DIM
        parts.append(jnp.dot(mix[:, lo:lo + FOURIER_GROUP_DIM], wf_ref[g], preferred_element_type=F32))
    yf = ((jnp.concatenate(parts, axis=-1) + bf_ref[...]) * gate_f).astype(BF16)
    out = x_ref[0].reshape(tm, D_MODEL)
    out = out + jnp.dot(ya, wout_ref[:ATTN_WIDTH, :], preferred_element_type=F32)
    out = out + jnp.dot(yf, wout_ref[ATTN_WIDTH:, :], preferred_element_type=F32)
    ms = jnp.mean(out * out, axis=-1, keepdims=True)
    o_ref[0] = (out * lax.rsqrt(ms + EPS) * fn_ref[...]).reshape(DFT_A, TILE_J, D_MODEL)


def _out_call(x4, ya, gates, y, g, wf, bf, wout, fnorm):
    B, _, S2, _ = x4.shape
    S = DFT_A * S2
    tm = ROW_TILE
    n_t = S // tm
    k1_rows = TILE_J * TILE_J
    n_k1_blocks = DFT_A // TILE_J
    const = lambda shape: pl.BlockSpec(shape, lambda i, b: (0,) * len(shape))
    scale = 1.0 / math.sqrt(S * FOURIER_GROUP_DIM)
    return pl.pallas_call(
        functools.partial(_out_kernel, scale=scale),
        grid=(n_t, B),
        in_specs=[
            pl.BlockSpec((1, DFT_A, TILE_J, D_MODEL), lambda i, b: (b, 0, i, 0)),
            pl.BlockSpec((1, tm, ATTN_WIDTH), lambda i, b: (b, i, 0)),
            pl.BlockSpec((1, tm, 2 * ATTN_WIDTH), lambda i, b: (b, i, 0)),
            pl.BlockSpec((1, n_t, 2, k1_rows, FOURIER_WIDTH), lambda i, b: (b, 0, 0, i % n_k1_blocks, 0)),
            pl.BlockSpec((1, tm, n_t * 2 * k1_rows), lambda i, b: (i, 0, 0)),
            const((N_FOURIER_GROUPS, FOURIER_GROUP_DIM, FOURIER_GROUP_DIM)),
            const((1, FOURIER_WIDTH)),
            const((D_MODEL, D_MODEL)),
            const((1, D_MODEL)),
        ],
        out_specs=pl.BlockSpec((1, DFT_A, TILE_J, D_MODEL), lambda i, b: (b, 0, i, 0)),
        out_shape=jax.ShapeDtypeStruct((B, DFT_A, S2, D_MODEL), F32),
        compiler_params=_params(("parallel", "parallel")),
        name="out_proj",
    )(x4, ya, gates, y, g, wf, bf, wout, fnorm)


def _tile_positions(seq_len):
    s2 = seq_len // DFT_A
    i = np.arange(s2 // TILE_J)[:, None, None]
    a = np.arange(DFT_A)[None, :, None]
    j = np.arange(TILE_J)[None, None, :]
    return (a * s2 + TILE_J * i + j).reshape(-1)


def _rope_tables(seq_len):
    pos = _tile_positions(seq_len)
    axis_dim = HEAD_DIM // 2
    inv_freq = ROPE_THETA ** (-np.arange(0, axis_dim, 2, dtype=np.float64) / axis_dim)
    row = (pos // GRID_W).astype(np.float64)
    col = (pos % GRID_W).astype(np.float64)
    ang = np.concatenate([row[:, None] * inv_freq, col[:, None] * inv_freq], axis=-1)
    cos = jnp.asarray(np.cos(ang).astype(np.float32))
    sin = jnp.asarray(np.sin(ang).astype(np.float32))
    cos_pair = jnp.repeat(cos, 2, axis=-1)
    sin_pair = jnp.stack([-sin, sin], axis=-1).reshape(seq_len, HEAD_DIM)
    return jnp.tile(cos_pair, (1, N_KV_HEADS)), jnp.tile(sin_pair, (1, N_KV_HEADS))


def _dft_cos_sin(n):
    idx = np.arange(n, dtype=np.int64)
    ang = ((idx[:, None] * idx[None, :]) % n).astype(np.float64) * (2.0 * math.pi / n)
    return np.cos(ang), np.sin(ang)


def _stage1_matrix():
    c, s = _dft_cos_sin(DFT_A)
    eye = np.eye(TILE_J)
    blocks = [[np.kron(c, eye), np.kron(-s, eye)], [np.kron(-s, eye), np.kron(-c, eye)]]
    return jnp.asarray(np.block(blocks).astype(np.float32)).astype(BF16)


def _expand_kernel(t_ref, r_ref, g_ref):
    g = jnp.dot(t_ref[0], r_ref[...], preferred_element_type=F32)
    row_j = lax.broadcasted_iota(jnp.int32, g.shape, 0) % TILE_J
    col_jj = (lax.broadcasted_iota(jnp.int32, g.shape, 1) // TILE_J) % TILE_J
    g_ref[0] = jnp.where(row_j == col_jj, g, 0.0).astype(BF16)


def _stage2_matrices(seq_len):
    s2_len = seq_len // DFT_A
    n_t = s2_len // TILE_J
    k = _tile_positions(seq_len).astype(np.int64)
    s2 = np.arange(s2_len, dtype=np.int64)
    ang = ((k[:, None] * s2[None, :]) % seq_len).astype(np.float64) * (2.0 * math.pi / seq_len)
    trig = np.stack([np.cos(ang), np.sin(ang)], axis=1)
    trig = trig.reshape(n_t, ROW_TILE, 2, n_t, TILE_J).transpose(0, 1, 3, 2, 4)
    n_in = n_t * 2 * TILE_J
    n_out = n_in * TILE_J
    compact = jnp.asarray(trig.reshape(n_t, ROW_TILE, n_in).astype(np.float32)).astype(BF16)
    copy_j = np.kron(np.ones((1, TILE_J)), np.eye(TILE_J))
    replicate = jnp.asarray(np.kron(np.eye(n_t * 2), copy_j), BF16)
    return pl.pallas_call(
        _expand_kernel,
        grid=(n_t,),
        in_specs=[
            pl.BlockSpec((1, ROW_TILE, n_in), lambda i: (i, 0, 0)),
            pl.BlockSpec((n_in, n_out), lambda i: (0, 0)),
        ],
        out_specs=pl.BlockSpec((1, ROW_TILE, n_out), lambda i: (i, 0, 0)),
        out_shape=jax.ShapeDtypeStruct((n_t, ROW_TILE, n_out), BF16),
        compiler_params=_params(("parallel",)),
        name="dft_stage2_matrix",
    )(compact, replicate)


def _trunk(x, lnw, win, gq, gk, hmean, cs, m1, wf, bf, wout, fnorm):
    B, S, _ = x.shape
    assert S % ROW_TILE == 0
    s2 = S // DFT_A
    x4 = x.reshape(B, DFT_A, s2, D_MODEL)
    cos_t, sin_t = _rope_tables(S)
    qT, k, vT, gates, y = _proj_call(x4, lnw, win, gq, gk, cos_t, sin_t, hmean, cs, m1)
    ya = _attn_call(qT, k, vT)
    g = _stage2_matrices(S)
    out = _out_call(x4, ya, gates, y, g, wf, bf, wout, fnorm)
    return out.reshape(B, S, D_MODEL)


def kernel(x_prompt, x_sample, ln_w, w_in, q_norm, k_norm, w_fourier, b_fourier, w_out, final_norm):
    assert ln_w.shape[0] == 1, "single mixer layer"
    lnw = ln_w[0].reshape(1, D_MODEL)
    win = w_in[0].astype(BF16)
    q_scale = HEAD_DIM ** -0.5 * math.log2(math.e)
    gq = jnp.tile(q_norm[0] * q_scale, N_Q_HEADS).reshape(1, ATTN_WIDTH)
    gk = jnp.tile(k_norm[0], N_KV_HEADS).reshape(1, KV_WIDTH)
    head_id = np.arange(ATTN_WIDTH) // HEAD_DIM
    hmean = jnp.asarray((head_id[:, None] == head_id[None, :]) / HEAD_DIM, BF16)
    c_c, s_c = _dft_cos_sin(FOURIER_GROUP_DIM)
    cs = jnp.asarray(np.concatenate([c_c, s_c], axis=1).astype(np.float32)).astype(BF16)
    m1 = _stage1_matrix()
    wf = w_fourier[0].astype(BF16)
    bf = b_fourier[0].reshape(1, FOURIER_WIDTH)
    wout = w_out[0].astype(BF16)
    fnorm = final_norm.reshape(1, D_MODEL)
    args = (lnw, win, gq, gk, hmean, cs, m1, wf, bf, wout, fnorm)
    return (_trunk(x_prompt, *args), _trunk(x_sample, *args))
```

```python
import functools
import math

import jax
import jax.numpy as jnp
import numpy as np
from jax import lax
from jax.experimental import pallas as pl
from jax.experimental.pallas import tpu as pltpu

D_MODEL = 1024
GRID_W = 64
ATTN_WIDTH = 512
FOURIER_WIDTH = 512
HEAD_DIM = 64
N_Q_HEADS = 8
N_KV_HEADS = 2
GQA_GROUP = 4
KV_WIDTH = 128
N_FOURIER_GROUPS = 4
FOURIER_GROUP_DIM = 128
ROPE_THETA = 10000.0
EPS = 1e-6
IN_WIDTH = 2304
Q0, K0, V0, GA0, U0, GF0 = 0, 512, 640, 768, 1280, 1792

VT_ROWS = HEAD_DIM + 16

DFT_A = 64
SUBLANES = 8
TILE_J = SUBLANES
ROW_TILE = DFT_A * TILE_J
Q_TILE = 256
VMEM_LIMIT_BYTES = 48 * 1024 * 1024
SCORE_BOUND_LOG2 = 64.0

F32 = jnp.float32
BF16 = jnp.bfloat16


def _params(semantics):
    return pltpu.CompilerParams(dimension_semantics=semantics, vmem_limit_bytes=VMEM_LIMIT_BYTES)


def _rope(x, cos, sin_signed):
    n = x.shape[-1]
    lane_is_even = (lax.broadcasted_iota(jnp.int32, x.shape, 1) & 1) == 0
    partner = jnp.where(lane_is_even, pltpu.roll(x, n - 1, 1), pltpu.roll(x, 1, 1))
    return x * cos + partner * sin_signed


def _proj_kernel(x_ref, lnw_ref, win_ref, gq_ref, gk_ref, cos_ref, sin_ref, hmean_ref, cs_ref, m1_ref,
                 qT_ref, k_ref, vT_ref, gate_ref, y_ref):
    tm = ROW_TILE
    x = x_ref[0].reshape(tm, D_MODEL)
    ms = jnp.mean(x * x, axis=-1, keepdims=True)
    h = (x * lax.rsqrt(ms + EPS) * lnw_ref[...]).astype(BF16)
    proj = jnp.dot(h, win_ref[...], preferred_element_type=F32)

    cos2 = cos_ref[...]
    sin2 = sin_ref[...]
    cos = jnp.concatenate([cos2] * (ATTN_WIDTH // KV_WIDTH), axis=1)
    sin = jnp.concatenate([sin2] * (ATTN_WIDTH // KV_WIDTH), axis=1)
    hmean = hmean_ref[...]

    q = proj[:, Q0:Q0 + ATTN_WIDTH]
    q_ms = jnp.dot((q * q).astype(BF16), hmean, preferred_element_type=F32)
    q = q * lax.rsqrt(q_ms + EPS) * gq_ref[...]
    q = _rope(q, cos, sin)
    qT = q.T.astype(BF16)
    for hd in range(N_Q_HEADS):
        j, hh = divmod(hd, GQA_GROUP)
        for t in range(tm // Q_TILE):
            qT_ref[0, j, t, :, hh * Q_TILE:(hh + 1) * Q_TILE] = (
                qT[hd * HEAD_DIM:(hd + 1) * HEAD_DIM, t * Q_TILE:(t + 1) * Q_TILE])

    k = proj[:, K0:K0 + KV_WIDTH]
    k_ms = jnp.dot((k * k).astype(BF16), hmean[:KV_WIDTH, :KV_WIDTH], preferred_element_type=F32)
    k = k * lax.rsqrt(k_ms + EPS) * gk_ref[...]
    k = _rope(k, cos2, sin2).astype(BF16)
    for j in range(N_KV_HEADS):
        k_ref[0, j] = k[:, j * HEAD_DIM:(j + 1) * HEAD_DIM]

    vT = proj[:, V0:V0 + KV_WIDTH].T.astype(BF16)
    ones = jnp.ones((VT_ROWS - HEAD_DIM, tm), BF16)
    for j in range(N_KV_HEADS):
        vT_ref[0, j, 0, :HEAD_DIM, :] = vT[j * HEAD_DIM:(j + 1) * HEAD_DIM]
        vT_ref[0, j, 0, HEAD_DIM:, :] = ones

    gate_ref[0, :, :ATTN_WIDTH] = jax.nn.silu(proj[:, GA0:GA0 + ATTN_WIDTH]).astype(BF16)
    gate_ref[0, :, ATTN_WIDTH:] = jax.nn.silu(proj[:, GF0:GF0 + FOURIER_WIDTH]).astype(BF16)

    cs = cs_ref[...]
    a_parts, b_parts = [], []
    for g in range(N_FOURIER_GROUPS):
        lo = g * FOURIER_GROUP_DIM
        u = proj[:, U0 + lo:U0 + lo + FOURIER_GROUP_DIM].astype(BF16)
        ab = jnp.dot(u, cs, preferred_element_type=F32)
        a_parts.append(ab[:, :FOURIER_GROUP_DIM].astype(BF16))
        b_parts.append(ab[:, FOURIER_GROUP_DIM:].astype(BF16))
    z = jnp.concatenate([jnp.concatenate(a_parts, axis=1), jnp.concatenate(b_parts, axis=1)], axis=0)
    y = jnp.dot(m1_ref[...], z, preferred_element_type=F32).astype(BF16)
    y_ref[0, 0, 0] = y[:tm]
    y_ref[0, 0, 1] = y[tm:]


def _proj_call(x4, lnw, win, gq, gk, cos_t, sin_t, hmean, cs, m1):
    B, _, S2, _ = x4.shape
    S = DFT_A * S2
    tm = ROW_TILE
    n_t = S // tm
    const = lambda shape: pl.BlockSpec(shape, lambda b, i: (0,) * len(shape))
    return pl.pallas_call(
        _proj_kernel,
        grid=(B, n_t),
        in_specs=[
            pl.BlockSpec((1, DFT_A, TILE_J, D_MODEL), lambda b, i: (b, 0, i, 0)),
            const((1, D_MODEL)),
            const((D_MODEL, IN_WIDTH)),
            const((1, ATTN_WIDTH)),
            const((1, KV_WIDTH)),
            pl.BlockSpec((tm, KV_WIDTH), lambda b, i: (i, 0)),
            pl.BlockSpec((tm, KV_WIDTH), lambda b, i: (i, 0)),
            const((ATTN_WIDTH, ATTN_WIDTH)),
            const((FOURIER_GROUP_DIM, 2 * FOURIER_GROUP_DIM)),
            const((2 * tm, 2 * tm)),
        ],
        out_specs=[
            pl.BlockSpec((1, N_KV_HEADS, tm // Q_TILE, HEAD_DIM, GQA_GROUP * Q_TILE), lambda b, i: (b, 0, i, 0, 0)),
            pl.BlockSpec((1, N_KV_HEADS, tm, HEAD_DIM), lambda b, i: (b, 0, i, 0)),
            pl.BlockSpec((1, N_KV_HEADS, 1, VT_ROWS, tm), lambda b, i: (b, 0, i, 0, 0)),
            pl.BlockSpec((1, tm, 2 * ATTN_WIDTH), lambda b, i: (b, i, 0)),
            pl.BlockSpec((1, 1, 2, tm, FOURIER_WIDTH), lambda b, i: (b, i, 0, 0, 0)),
        ],
        out_shape=[
            jax.ShapeDtypeStruct((B, N_KV_HEADS, S // Q_TILE, HEAD_DIM, GQA_GROUP * Q_TILE), BF16),
            jax.ShapeDtypeStruct((B, N_KV_HEADS, S, HEAD_DIM), BF16),
            jax.ShapeDtypeStruct((B, N_KV_HEADS, n_t, VT_ROWS, tm), BF16),
            jax.ShapeDtypeStruct((B, S, 2 * ATTN_WIDTH), BF16),
            jax.ShapeDtypeStruct((B, n_t, 2, tm, FOURIER_WIDTH), BF16),
        ],
        compiler_params=_params(("parallel", "parallel")),
        name="proj",
    )(x4, lnw, win, gq, gk, cos_t, sin_t, hmean, cs, m1)


def _attn_kernel(qT_ref, k_ref, vT_ref, o_ref, m_ref, acc_ref, s_ref, cmax_ref, *, n_chunks, tk, unroll):
    n_q = qT_ref.shape[2]
    tq = qT_ref.shape[4] // GQA_GROUP

    def scores(t, c, slot):
        off = pl.multiple_of(c * tk, tk)
        s = jnp.dot(k_ref[0, 0, pl.ds(off, tk), :], qT_ref[0, 0, t], preferred_element_type=F32)
        s_ref[slot] = s
        cmax_ref[slot, 0:1] = jnp.max(s, axis=0, keepdims=True)

    def accumulate(c, slot):
        m_old = m_ref[0:1]
        m_new = jnp.maximum(m_old, cmax_ref[slot, 0:1])
        p = jnp.exp2(s_ref[slot] - m_new).astype(BF16)
        alpha = jnp.exp2(m_old - m_new)
        pv = jnp.dot(vT_ref[0, 0, c], p, preferred_element_type=F32)
        acc_ref[...] = alpha * acc_ref[...] + pv
        m_ref[0:1] = m_new

    scores(0, 0, 0)

    def tile(t, carry):
        m_ref[0:1] = jnp.full((1, m_ref.shape[1]), -jnp.inf, F32)
        acc_ref[...] = jnp.zeros(acc_ref.shape, F32)

        def group(i, inner):
            c = unroll * i
            for u in range(unroll):
                scores(t, c + u + 1, (u + 1) % 2)
                accumulate(c + u, u % 2)
            return inner

        lax.fori_loop(0, n_chunks // unroll - 1, group, 0)
        c = n_chunks - unroll
        for u in range(unroll):
            if u < unroll - 1:
                scores(t, c + u + 1, (u + 1) % 2)
            else:
                scores(jnp.minimum(t + 1, n_q - 1), 0, 0)
            accumulate(c + u, u % 2)
        acc = acc_ref[...]
        oT = acc[:HEAD_DIM] / acc[HEAD_DIM:HEAD_DIM + 1]
        oT = jnp.concatenate([oT[:, hh * tq:(hh + 1) * tq] for hh in range(GQA_GROUP)], axis=0)
        o_ref[0, pl.ds(pl.multiple_of(t * tq, tq), tq), :] = oT.T.astype(BF16)
        return carry

    lax.fori_loop(0, n_q, tile, 0)


def _attn_bounded_kernel(qT_ref, k_ref, vT_ref, o_ref, *, n_chunks, tk):
    n_q = qT_ref.shape[2]
    tq = qT_ref.shape[4] // GQA_GROUP

    def tile(t, carry):
        acc = None
        for c in range(n_chunks):
            s = jnp.dot(k_ref[0, 0, c * tk:(c + 1) * tk, :], qT_ref[0, 0, t], preferred_element_type=F32)
            pv = jnp.dot(vT_ref[0, 0, c], jnp.exp2(s).astype(BF16), preferred_element_type=F32)
            acc = pv if acc is None else acc + pv
        oT = acc[:HEAD_DIM] / acc[HEAD_DIM:HEAD_DIM + 1]
        oT = jnp.concatenate([oT[:, hh * tq:(hh + 1) * tq] for hh in range(GQA_GROUP)], axis=0)
        o_ref[0, pl.ds(pl.multiple_of(t * tq, tq), tq), :] = oT.T.astype(BF16)
        return carry

    lax.fori_loop(0, n_q, tile, 0)


def _attn_call(qT, k, vT, bounded):
    B, _, n_q, _, _ = qT.shape
    S = k.shape[2]
    n_chunks, tk = vT.shape[2], vT.shape[4]
    unroll = 4 if n_chunks >= 16 else 2
    assert n_chunks % unroll == 0 and n_chunks >= 2 * unroll
    tq = Q_TILE
    width = GQA_GROUP * HEAD_DIM
    if bounded:
        body = functools.partial(_attn_bounded_kernel, n_chunks=n_chunks, tk=tk)
        scratch = []
    else:
        body = functools.partial(_attn_kernel, n_chunks=n_chunks, tk=tk, unroll=unroll)
        scratch = [pltpu.VMEM((SUBLANES, GQA_GROUP * tq), F32), pltpu.VMEM((VT_ROWS, GQA_GROUP * tq), F32),
                   pltpu.VMEM((2, tk, GQA_GROUP * tq), F32), pltpu.VMEM((2, SUBLANES, GQA_GROUP * tq), F32)]
    return pl.pallas_call(
        body,
        grid=(B, N_KV_HEADS),
        in_specs=[
            pl.BlockSpec((1, 1, n_q, HEAD_DIM, GQA_GROUP * tq), lambda b, j: (b, j, 0, 0, 0)),
            pl.BlockSpec((1, 1, S, HEAD_DIM), lambda b, j: (b, j, 0, 0)),
            pl.BlockSpec((1, 1, n_chunks, VT_ROWS, tk), lambda b, j: (b, j, 0, 0, 0)),
        ],
        out_specs=pl.BlockSpec((1, S, width), lambda b, j: (b, 0, j)),
        out_shape=jax.ShapeDtypeStruct((B, S, ATTN_WIDTH), BF16),
        scratch_shapes=scratch,
        compiler_params=_params(("parallel", "parallel")),
        name="attn_bounded" if bounded else "attn",
    )(qT, k, vT)


def _out_kernel(x_ref, ya_ref, gate_ref, y_ref, g_ref, wf_ref, bf_ref, wout_ref, fn_ref, o_ref, *, scale):
    tm = ROW_TILE
    gate_a = gate_ref[0, :, :ATTN_WIDTH].astype(F32)
    gate_f = gate_ref[0, :, ATTN_WIDTH:].astype(F32)
    ya = (ya_ref[0].astype(F32) * gate_a).astype(BF16)
    y = y_ref[0].reshape(g_ref.shape[2], FOURIER_WIDTH)
    mix = (jnp.dot(g_ref[0], y, preferred_element_type=F32) * scale).astype(BF16)
    parts = []
    for g in range(N_FOURIER_GROUPS):
        lo = g * FOURIER_GROUP_DIM
        parts.append(jnp.dot(mix[:, lo:lo + FOURIER_GROUP_DIM], wf_ref[g], preferred_element_type=F32))
    yf = ((jnp.concatenate(parts, axis=-1) + bf_ref[...]) * gate_f).astype(BF16)
    out = x_ref[0].reshape(tm, D_MODEL)
    out = out + jnp.dot(ya, wout_ref[:ATTN_WIDTH, :], preferred_element_type=F32)
    out = out + jnp.dot(yf, wout_ref[ATTN_WIDTH:, :], preferred_element_type=F32)
    ms = jnp.mean(out * out, axis=-1, keepdims=True)
    o_ref[0] = (out * lax.rsqrt(ms + EPS) * fn_ref[...]).reshape(DFT_A, TILE_J, D_MODEL)


def _out_call(x4, ya, gates, y, g, wf, bf, wout, fnorm):
    B, _, S2, _ = x4.shape
    S = DFT_A * S2
    tm = ROW_TILE
    n_t = S // tm
    k1_rows = TILE_J * TILE_J
    n_k1_blocks = DFT_A // TILE_J
    const = lambda shape: pl.BlockSpec(shape, lambda i, b: (0,) * len(shape))
    scale = 1.0 / math.sqrt(S * FOURIER_GROUP_DIM)
    return pl.pallas_call(
        functools.partial(_out_kernel, scale=scale),
        grid=(n_t, B),
        in_specs=[
            pl.BlockSpec((1, DFT_A, TILE_J, D_MODEL), lambda i, b: (b, 0, i, 0)),
            pl.BlockSpec((1, tm, ATTN_WIDTH), lambda i, b: (b, i, 0)),
            pl.BlockSpec((1, tm, 2 * ATTN_WIDTH), lambda i, b: (b, i, 0)),
            pl.BlockSpec((1, n_t, 2, k1_rows, FOURIER_WIDTH), lambda i, b: (b, 0, 0, i % n_k1_blocks, 0)),
            pl.BlockSpec((1, tm, n_t * 2 * k1_rows), lambda i, b: (i, 0, 0)),
            const((N_FOURIER_GROUPS, FOURIER_GROUP_DIM, FOURIER_GROUP_DIM)),
            const((1, FOURIER_WIDTH)),
            const((D_MODEL, D_MODEL)),
            const((1, D_MODEL)),
        ],
        out_specs=pl.BlockSpec((1, DFT_A, TILE_J, D_MODEL), lambda i, b: (b, 0, i, 0)),
        out_shape=jax.ShapeDtypeStruct((B, DFT_A, S2, D_MODEL), F32),
        compiler_params=_params(("parallel", "parallel")),
        name="out_proj",
    )(x4, ya, gates, y, g, wf, bf, wout, fnorm)


def _tile_positions(seq_len):
    s2 = seq_len // DFT_A
    i = np.arange(s2 // TILE_J)[:, None, None]
    a = np.arange(DFT_A)[None, :, None]
    j = np.arange(TILE_J)[None, None, :]
    return (a * s2 + TILE_J * i + j).reshape(-1)


def _rope_tables(seq_len):
    pos = _tile_positions(seq_len)
    axis_dim = HEAD_DIM // 2
    inv_freq = ROPE_THETA ** (-np.arange(0, axis_dim, 2, dtype=np.float64) / axis_dim)
    row = (pos // GRID_W).astype(np.float64)
    col = (pos % GRID_W).astype(np.float64)
    ang = np.concatenate([row[:, None] * inv_freq, col[:, None] * inv_freq], axis=-1)
    cos = jnp.asarray(np.cos(ang).astype(np.float32))
    sin = jnp.asarray(np.sin(ang).astype(np.float32))
    cos_pair = jnp.repeat(cos, 2, axis=-1)
    sin_pair = jnp.stack([-sin, sin], axis=-1).reshape(seq_len, HEAD_DIM)
    return jnp.tile(cos_pair, (1, N_KV_HEADS)), jnp.tile(sin_pair, (1, N_KV_HEADS))


def _dft_cos_sin(n):
    idx = np.arange(n, dtype=np.int64)
    ang = ((idx[:, None] * idx[None, :]) % n).astype(np.float64) * (2.0 * math.pi / n)
    return np.cos(ang), np.sin(ang)


def _stage1_matrix():
    c, s = _dft_cos_sin(DFT_A)
    eye = np.eye(TILE_J)
    blocks = [[np.kron(c, eye), np.kron(-s, eye)], [np.kron(-s, eye), np.kron(-c, eye)]]
    return jnp.asarray(np.block(blocks).astype(np.float32)).astype(BF16)


def _expand_kernel(t_ref, r_ref, g_ref):
    g = jnp.dot(t_ref[0], r_ref[...], preferred_element_type=F32)
    row_j = lax.broadcasted_iota(jnp.int32, g.shape, 0) % TILE_J
    col_jj = (lax.broadcasted_iota(jnp.int32, g.shape, 1) // TILE_J) % TILE_J
    g_ref[0] = jnp.where(row_j == col_jj, g, 0.0).astype(BF16)


def _stage2_matrices(seq_len):
    s2_len = seq_len // DFT_A
    n_t = s2_len // TILE_J
    k = _tile_positions(seq_len).astype(np.int64)
    s2 = np.arange(s2_len, dtype=np.int64)
    ang = ((k[:, None] * s2[None, :]) % seq_len).astype(np.float64) * (2.0 * math.pi / seq_len)
    trig = np.stack([np.cos(ang), np.sin(ang)], axis=1)
    trig = trig.reshape(n_t, ROW_TILE, 2, n_t, TILE_J).transpose(0, 1, 3, 2, 4)
    n_in = n_t * 2 * TILE_J
    n_out = n_in * TILE_J
    compact = jnp.asarray(trig.reshape(n_t, ROW_TILE, n_in).astype(np.float32)).astype(BF16)
    copy_j = np.kron(np.ones((1, TILE_J)), np.eye(TILE_J))
    replicate = jnp.asarray(np.kron(np.eye(n_t * 2), copy_j), BF16)
    return pl.pallas_call(
        _expand_kernel,
        grid=(n_t,),
        in_specs=[
            pl.BlockSpec((1, ROW_TILE, n_in), lambda i: (i, 0, 0)),
            pl.BlockSpec((n_in, n_out), lambda i: (0, 0)),
        ],
        out_specs=pl.BlockSpec((1, ROW_TILE, n_out), lambda i: (i, 0, 0)),
        out_shape=jax.ShapeDtypeStruct((n_t, ROW_TILE, n_out), BF16),
        compiler_params=_params(("parallel",)),
        name="dft_stage2_matrix",
    )(compact, replicate)


def _trunk(x, score_bound, lnw, win, gq, gk, hmean, cs, m1, wf, bf, wout, fnorm):
    B, S, _ = x.shape
    assert S % ROW_TILE == 0
    s2 = S // DFT_A
    x4 = x.reshape(B, DFT_A, s2, D_MODEL)
    cos_t, sin_t = _rope_tables(S)
    qT, k, vT, gates, y = _proj_call(x4, lnw, win, gq, gk, cos_t, sin_t, hmean, cs, m1)
    ya = lax.cond(score_bound <= SCORE_BOUND_LOG2,
                  functools.partial(_attn_call, bounded=True),
                  functools.partial(_attn_call, bounded=False), qT, k, vT)
    g = _stage2_matrices(S)
    out = _out_call(x4, ya, gates, y, g, wf, bf, wout, fnorm)
    return out.reshape(B, S, D_MODEL)


def kernel(x_prompt, x_sample, ln_w, w_in, q_norm, k_norm, w_fourier, b_fourier, w_out, final_norm):
    assert ln_w.shape[0] == 1, "single mixer layer"
    lnw = ln_w[0].reshape(1, D_MODEL)
    win = w_in[0].astype(BF16)
    q_scale = HEAD_DIM ** -0.5 * math.log2(math.e)
    gq = jnp.tile(q_norm[0] * q_scale, N_Q_HEADS).reshape(1, ATTN_WIDTH)
    gk = jnp.tile(k_norm[0], N_KV_HEADS).reshape(1, KV_WIDTH)
    score_bound = HEAD_DIM * jnp.max(jnp.abs(q_norm[0] * q_scale)) * jnp.max(jnp.abs(k_norm[0]))
    head_id = np.arange(ATTN_WIDTH) // HEAD_DIM
    hmean = jnp.asarray((head_id[:, None] == head_id[None, :]) / HEAD_DIM, BF16)
    c_c, s_c = _dft_cos_sin(FOURIER_GROUP_DIM)
    cs = jnp.asarray(np.concatenate([c_c, s_c], axis=1).astype(np.float32)).astype(BF16)
    m1 = _stage1_matrix()
    wf = w_fourier[0].astype(BF16)
    bf = b_fourier[0].reshape(1, FOURIER_WIDTH)
    wout = w_out[0].astype(BF16)
    fnorm = final_norm.reshape(1, D_MODEL)
    args = (lnw, win, gq, gk, hmean, cs, m1, wf, bf, wout, fnorm)
    return (_trunk(x_prompt, score_bound, *args), _trunk(x_sample, score_bound, *args))
```

```python
import functools
import math

import jax
import jax.numpy as jnp
import numpy as np
from jax import lax
from jax.experimental import pallas as pl
from jax.experimental.pallas import tpu as pltpu

D_MODEL = 1024
GRID_W = 64
ATTN_WIDTH = 512
FOURIER_WIDTH = 512
HEAD_DIM = 64
N_Q_HEADS = 8
N_KV_HEADS = 2
GQA_GROUP = 4
KV_WIDTH = 128
N_FOURIER_GROUPS = 4
FOURIER_GROUP_DIM = 128
ROPE_THETA = 10000.0
EPS = 1e-6
IN_WIDTH = 2304
Q0, K0, V0, GA0, U0, GF0 = 0, 512, 640, 768, 1280, 1792

VT_ROWS = HEAD_DIM + 16

DFT_A = 64
SUBLANES = 8
TILE_J = SUBLANES
ROW_TILE = DFT_A * TILE_J
Q_TILE = 512
VMEM_LIMIT_BYTES = 48 * 1024 * 1024
SCORE_BOUND_LOG2 = 64.0
BOUNDED_CHUNKS_PER_TRIP = 16

F32 = jnp.float32
BF16 = jnp.bfloat16


def _params(semantics):
    return pltpu.CompilerParams(dimension_semantics=semantics, vmem_limit_bytes=VMEM_LIMIT_BYTES)


def _rope(x, cos, sin_signed):
    n = x.shape[-1]
    lane_is_even = (lax.broadcasted_iota(jnp.int32, x.shape, 1) & 1) == 0
    partner = jnp.where(lane_is_even, pltpu.roll(x, n - 1, 1), pltpu.roll(x, 1, 1))
    return x * cos + partner * sin_signed


def _proj_kernel(x_ref, lnw_ref, win_ref, gq_ref, gk_ref, cos_ref, sin_ref, hmean_ref, cs_ref, m1_ref,
                 qT_ref, k_ref, vT_ref, gate_ref, y_ref):
    tm = ROW_TILE
    x = x_ref[0].reshape(tm, D_MODEL)
    ms = jnp.mean(x * x, axis=-1, keepdims=True)
    h = (x * lax.rsqrt(ms + EPS) * lnw_ref[...]).astype(BF16)
    proj = jnp.dot(h, win_ref[...], preferred_element_type=F32)

    cos2 = cos_ref[...]
    sin2 = sin_ref[...]
    cos = jnp.concatenate([cos2] * (ATTN_WIDTH // KV_WIDTH), axis=1)
    sin = jnp.concatenate([sin2] * (ATTN_WIDTH // KV_WIDTH), axis=1)
    hmean = hmean_ref[...]

    q = proj[:, Q0:Q0 + ATTN_WIDTH]
    q_ms = jnp.dot((q * q).astype(BF16), hmean, preferred_element_type=F32)
    q = q * lax.rsqrt(q_ms + EPS) * gq_ref[...]
    q = _rope(q, cos, sin)
    qT = q.T.astype(BF16)
    for hd in range(N_Q_HEADS):
        j, hh = divmod(hd, GQA_GROUP)
        for t in range(tm // Q_TILE):
            qT_ref[0, j, t, :, hh * Q_TILE:(hh + 1) * Q_TILE] = (
                qT[hd * HEAD_DIM:(hd + 1) * HEAD_DIM, t * Q_TILE:(t + 1) * Q_TILE])

    k = proj[:, K0:K0 + KV_WIDTH]
    k_ms = jnp.dot((k * k).astype(BF16), hmean[:KV_WIDTH, :KV_WIDTH], preferred_element_type=F32)
    k = k * lax.rsqrt(k_ms + EPS) * gk_ref[...]
    k = _rope(k, cos2, sin2).astype(BF16)
    for j in range(N_KV_HEADS):
        k_ref[0, j] = k[:, j * HEAD_DIM:(j + 1) * HEAD_DIM]

    vT = proj[:, V0:V0 + KV_WIDTH].T.astype(BF16)
    ones = jnp.ones((VT_ROWS - HEAD_DIM, tm), BF16)
    for j in range(N_KV_HEADS):
        vT_ref[0, j, 0, :HEAD_DIM, :] = vT[j * HEAD_DIM:(j + 1) * HEAD_DIM]
        vT_ref[0, j, 0, HEAD_DIM:, :] = ones

    gate_ref[0, :, :ATTN_WIDTH] = jax.nn.silu(proj[:, GA0:GA0 + ATTN_WIDTH]).astype(BF16)
    gate_ref[0, :, ATTN_WIDTH:] = jax.nn.silu(proj[:, GF0:GF0 + FOURIER_WIDTH]).astype(BF16)

    u = proj[:, U0:U0 + FOURIER_WIDTH].astype(BF16)
    pq = jnp.dot(m1_ref[...], u, preferred_element_type=F32).astype(BF16)
    cs = cs_ref[...]
    for g in range(N_FOURIER_GROUPS):
        cols = slice(g * FOURIER_GROUP_DIM, (g + 1) * FOURIER_GROUP_DIM)
        y = jnp.dot(jnp.concatenate([pq[:tm, cols], pq[tm:, cols]], axis=1), cs, preferred_element_type=F32)
        y_ref[0, 0, 0, :, cols] = y[:, :FOURIER_GROUP_DIM].astype(BF16)
        y_ref[0, 0, 1, :, cols] = y[:, FOURIER_GROUP_DIM:].astype(BF16)


def _proj_call(x4, lnw, win, gq, gk, cos_t, sin_t, hmean, cs, m1):
    B, _, S2, _ = x4.shape
    S = DFT_A * S2
    tm = ROW_TILE
    n_t = S // tm
    const = lambda shape: pl.BlockSpec(shape, lambda b, i: (0,) * len(shape))
    return pl.pallas_call(
        _proj_kernel,
        grid=(B, n_t),
        in_specs=[
            pl.BlockSpec((1, DFT_A, TILE_J, D_MODEL), lambda b, i: (b, 0, i, 0)),
            const((1, D_MODEL)),
            const((D_MODEL, IN_WIDTH)),
            const((1, ATTN_WIDTH)),
            const((1, KV_WIDTH)),
            pl.BlockSpec((tm, KV_WIDTH), lambda b, i: (i, 0)),
            pl.BlockSpec((tm, KV_WIDTH), lambda b, i: (i, 0)),
            const((ATTN_WIDTH, ATTN_WIDTH)),
            const((2 * FOURIER_GROUP_DIM, 2 * FOURIER_GROUP_DIM)),
            const((2 * tm, tm)),
        ],
        out_specs=[
            pl.BlockSpec((1, N_KV_HEADS, tm // Q_TILE, HEAD_DIM, GQA_GROUP * Q_TILE), lambda b, i: (b, 0, i, 0, 0)),
            pl.BlockSpec((1, N_KV_HEADS, tm, HEAD_DIM), lambda b, i: (b, 0, i, 0)),
            pl.BlockSpec((1, N_KV_HEADS, 1, VT_ROWS, tm), lambda b, i: (b, 0, i, 0, 0)),
            pl.BlockSpec((1, tm, 2 * ATTN_WIDTH), lambda b, i: (b, i, 0)),
            pl.BlockSpec((1, 1, 2, tm, FOURIER_WIDTH), lambda b, i: (b, i, 0, 0, 0)),
        ],
        out_shape=[
            jax.ShapeDtypeStruct((B, N_KV_HEADS, S // Q_TILE, HEAD_DIM, GQA_GROUP * Q_TILE), BF16),
            jax.ShapeDtypeStruct((B, N_KV_HEADS, S, HEAD_DIM), BF16),
            jax.ShapeDtypeStruct((B, N_KV_HEADS, n_t, VT_ROWS, tm), BF16),
            jax.ShapeDtypeStruct((B, S, 2 * ATTN_WIDTH), BF16),
            jax.ShapeDtypeStruct((B, n_t, 2, tm, FOURIER_WIDTH), BF16),
        ],
        compiler_params=_params(("parallel", "parallel")),
        name="proj",
    )(x4, lnw, win, gq, gk, cos_t, sin_t, hmean, cs, m1)


def _attn_kernel(qT_ref, k_ref, vT_ref, o_ref, m_ref, acc_ref, s_ref, cmax_ref, *, n_chunks, tk, unroll):
    n_q = qT_ref.shape[2]
    tq = qT_ref.shape[4] // GQA_GROUP

    def scores(t, c, slot):
        off = pl.multiple_of(c * tk, tk)
        s = jnp.dot(k_ref[0, 0, pl.ds(off, tk), :], qT_ref[0, 0, t], preferred_element_type=F32)
        s_ref[slot] = s
        cmax_ref[slot, 0:1] = jnp.max(s, axis=0, keepdims=True)

    def accumulate(c, slot):
        m_old = m_ref[0:1]
        m_new = jnp.maximum(m_old, cmax_ref[slot, 0:1])
        p = jnp.exp2(s_ref[slot] - m_new).astype(BF16)
        alpha = jnp.exp2(m_old - m_new)
        pv = jnp.dot(vT_ref[0, 0, c], p, preferred_element_type=F32)
        acc_ref[...] = alpha * acc_ref[...] + pv
        m_ref[0:1] = m_new

    scores(0, 0, 0)

    def tile(t, carry):
        m_ref[0:1] = jnp.full((1, m_ref.shape[1]), -jnp.inf, F32)
        acc_ref[...] = jnp.zeros(acc_ref.shape, F32)

        def group(i, inner):
            c = unroll * i
            for u in range(unroll):
                scores(t, c + u + 1, (u + 1) % 2)
                accumulate(c + u, u % 2)
            return inner

        lax.fori_loop(0, n_chunks // unroll - 1, group, 0)
        c = n_chunks - unroll
        for u in range(unroll):
            if u < unroll - 1:
                scores(t, c + u + 1, (u + 1) % 2)
            else:
                scores(jnp.minimum(t + 1, n_q - 1), 0, 0)
            accumulate(c + u, u % 2)
        acc = acc_ref[...]
        oT = acc[:HEAD_DIM] / acc[HEAD_DIM:HEAD_DIM + 1]
        oT = jnp.concatenate([oT[:, hh * tq:(hh + 1) * tq] for hh in range(GQA_GROUP)], axis=0)
        o_ref[0, pl.ds(pl.multiple_of(t * tq, tq), tq), :] = oT.T.astype(BF16)
        return carry

    lax.fori_loop(0, n_q, tile, 0)


def _attn_bounded_kernel(qT_ref, k_ref, vT_ref, o_ref, *, n_chunks, tk):
    n_q = qT_ref.shape[2]
    tq = qT_ref.shape[4] // GQA_GROUP

    def tile(t, carry):
        acc = None
        for c in range(n_chunks):
            s = jnp.dot(k_ref[0, 0, c * tk:(c + 1) * tk, :], qT_ref[0, 0, t], preferred_element_type=F32)
            pv = jnp.dot(vT_ref[0, 0, c], jnp.exp2(s).astype(BF16), preferred_element_type=F32)
            acc = pv if acc is None else acc + pv
        oT = acc[:HEAD_DIM] / acc[HEAD_DIM:HEAD_DIM + 1]
        oT = jnp.concatenate([oT[:, hh * tq:(hh + 1) * tq] for hh in range(GQA_GROUP)], axis=0)
        o_ref[0, pl.ds(pl.multiple_of(t * tq, tq), tq), :] = oT.T.astype(BF16)
        return carry

    tiles_per_trip = max(1, BOUNDED_CHUNKS_PER_TRIP // n_chunks)
    assert n_q % tiles_per_trip == 0

    def trip(i, carry):
        for u in range(tiles_per_trip):
            tile(tiles_per_trip * i + u, carry)
        return carry

    lax.fori_loop(0, n_q // tiles_per_trip, trip, 0)


def _attn_call(qT, k, vT, bounded):
    B, _, n_q, _, _ = qT.shape
    S = k.shape[2]
    n_chunks, tk = vT.shape[2], vT.shape[4]
    unroll = 4 if n_chunks >= 16 else 2
    assert n_chunks % unroll == 0 and n_chunks >= 2 * unroll
    tq = Q_TILE
    width = GQA_GROUP * HEAD_DIM
    if bounded:
        body = functools.partial(_attn_bounded_kernel, n_chunks=n_chunks, tk=tk)
        scratch = []
    else:
        body = functools.partial(_attn_kernel, n_chunks=n_chunks, tk=tk, unroll=unroll)
        scratch = [pltpu.VMEM((SUBLANES, GQA_GROUP * tq), F32), pltpu.VMEM((VT_ROWS, GQA_GROUP * tq), F32),
                   pltpu.VMEM((2, tk, GQA_GROUP * tq), F32), pltpu.VMEM((2, SUBLANES, GQA_GROUP * tq), F32)]
    return pl.pallas_call(
        body,
        grid=(B, N_KV_HEADS),
        in_specs=[
            pl.BlockSpec((1, 1, n_q, HEAD_DIM, GQA_GROUP * tq), lambda b, j: (b, j, 0, 0, 0)),
            pl.BlockSpec((1, 1, S, HEAD_DIM), lambda b, j: (b, j, 0, 0)),
            pl.BlockSpec((1, 1, n_chunks, VT_ROWS, tk), lambda b, j: (b, j, 0, 0, 0)),
        ],
        out_specs=pl.BlockSpec((1, S, width), lambda b, j: (b, 0, j)),
        out_shape=jax.ShapeDtypeStruct((B, S, ATTN_WIDTH), BF16),
        scratch_shapes=scratch,
        compiler_params=_params(("parallel", "parallel")),
        name="attn_bounded" if bounded else "attn",
    )(qT, k, vT)


def _out_kernel(x_ref, ya_ref, gate_ref, y_ref, g_ref, wf_ref, bf_ref, wout_ref, fn_ref, o_ref, *, scale):
    tm = ROW_TILE
    gate_a = gate_ref[0, :, :ATTN_WIDTH].astype(F32)
    gate_f = gate_ref[0, :, ATTN_WIDTH:].astype(F32)
    ya = (ya_ref[0].astype(F32) * gate_a).astype(BF16)
    y = y_ref[0].reshape(g_ref.shape[2], FOURIER_WIDTH)
    mix = (jnp.dot(g_ref[0], y, preferred_element_type=F32) * scale).astype(BF16)
    parts = []
    for g in range(N_FOURIER_GROUPS):
        lo = g * FOURIER_GROUP_DIM
        parts.append(jnp.dot(mix[:, lo:lo + FOURIER_GROUP_DIM], wf_ref[g], preferred_element_type=F32))
    yf = ((jnp.concatenate(parts, axis=-1) + bf_ref[...]) * gate_f).astype(BF16)
    out = x_ref[0].reshape(tm, D_MODEL)
    out = out + jnp.dot(ya, wout_ref[:ATTN_WIDTH, :], preferred_element_type=F32)
    out = out + jnp.dot(yf, wout_ref[ATTN_WIDTH:, :], preferred_element_type=F32)
    ms = jnp.mean(out * out, axis=-1, keepdims=True)
    o_ref[0] = (out * lax.rsqrt(ms + EPS) * fn_ref[...]).reshape(DFT_A, TILE_J, D_MODEL)


def _out_call(x4, ya, gates, y, g, wf, bf, wout, fnorm):
    B, _, S2, _ = x4.shape
    S = DFT_A * S2
    tm = ROW_TILE
    n_t = S // tm
    k1_rows = TILE_J * TILE_J
    n_k1_blocks = DFT_A // TILE_J
    const = lambda shape: pl.BlockSpec(shape, lambda i, b: (0,) * len(shape))
    scale = 1.0 / math.sqrt(S * FOURIER_GROUP_DIM)
    return pl.pallas_call(
        functools.partial(_out_kernel, scale=scale),
        grid=(n_t, B),
        in_specs=[
            pl.BlockSpec((1, DFT_A, TILE_J, D_MODEL), lambda i, b: (b, 0, i, 0)),
            pl.BlockSpec((1, tm, ATTN_WIDTH), lambda i, b: (b, i, 0)),
            pl.BlockSpec((1, tm, 2 * ATTN_WIDTH), lambda i, b: (b, i, 0)),
            pl.BlockSpec((1, n_t, 2, k1_rows, FOURIER_WIDTH), lambda i, b: (b, 0, 0, i % n_k1_blocks, 0)),
            pl.BlockSpec((1, tm, n_t * 2 * k1_rows), lambda i, b: (i, 0, 0)),
            const((N_FOURIER_GROUPS, FOURIER_GROUP_DIM, FOURIER_GROUP_DIM)),
            const((1, FOURIER_WIDTH)),
            const((D_MODEL, D_MODEL)),
            const((1, D_MODEL)),
        ],
        out_specs=pl.BlockSpec((1, DFT_A, TILE_J, D_MODEL), lambda i, b: (b, 0, i, 0)),
        out_shape=jax.ShapeDtypeStruct((B, DFT_A, S2, D_MODEL), F32),
        compiler_params=_params(("parallel", "parallel")),
        name="out_proj",
    )(x4, ya, gates, y, g, wf, bf, wout, fnorm)


def _tile_positions(seq_len):
    s2 = seq_len // DFT_A
    i = np.arange(s2 // TILE_J)[:, None, None]
    a = np.arange(DFT_A)[None, :, None]
    j = np.arange(TILE_J)[None, None, :]
    return (a * s2 + TILE_J * i + j).reshape(-1)


def _rope_tables(seq_len):
    pos = _tile_positions(seq_len)
    axis_dim = HEAD_DIM // 2
    inv_freq = ROPE_THETA ** (-np.arange(0, axis_dim, 2, dtype=np.float64) / axis_dim)
    row = (pos // GRID_W).astype(np.float64)
    col = (pos % GRID_W).astype(np.float64)
    ang = np.concatenate([row[:, None] * inv_freq, col[:, None] * inv_freq], axis=-1)
    cos = jnp.asarray(np.cos(ang).astype(np.float32))
    sin = jnp.asarray(np.sin(ang).astype(np.float32))
    cos_pair = jnp.repeat(cos, 2, axis=-1)
    sin_pair = jnp.stack([-sin, sin], axis=-1).reshape(seq_len, HEAD_DIM)
    return jnp.tile(cos_pair, (1, N_KV_HEADS)), jnp.tile(sin_pair, (1, N_KV_HEADS))


def _dft_cos_sin(n):
    idx = np.arange(n, dtype=np.int64)
    ang = ((idx[:, None] * idx[None, :]) % n).astype(np.float64) * (2.0 * math.pi / n)
    return np.cos(ang), np.sin(ang)


def _stage1_matrix():
    c, s = _dft_cos_sin(DFT_A)
    eye = np.eye(TILE_J)
    stacked = np.concatenate([np.kron(c, eye), np.kron(s, eye)], axis=0)
    return jnp.asarray(stacked.astype(np.float32)).astype(BF16)


def _expand_kernel(t_ref, r_ref, g_ref):
    g = jnp.dot(t_ref[0], r_ref[...], preferred_element_type=F32)
    row_j = lax.broadcasted_iota(jnp.int32, g.shape, 0) % TILE_J
    col_jj = (lax.broadcasted_iota(jnp.int32, g.shape, 1) // TILE_J) % TILE_J
    g_ref[0] = jnp.where(row_j == col_jj, g, 0.0).astype(BF16)


def _stage2_matrices(seq_len):
    s2_len = seq_len // DFT_A
    n_t = s2_len // TILE_J
    k = _tile_positions(seq_len).astype(np.int64)
    s2 = np.arange(s2_len, dtype=np.int64)
    ang = ((k[:, None] * s2[None, :]) % seq_len).astype(np.float64) * (2.0 * math.pi / seq_len)
    trig = np.stack([np.cos(ang), np.sin(ang)], axis=1)
    trig = trig.reshape(n_t, ROW_TILE, 2, n_t, TILE_J).transpose(0, 1, 3, 2, 4)
    n_in = n_t * 2 * TILE_J
    n_out = n_in * TILE_J
    compact = jnp.asarray(trig.reshape(n_t, ROW_TILE, n_in).astype(np.float32)).astype(BF16)
    copy_j = np.kron(np.ones((1, TILE_J)), np.eye(TILE_J))
    replicate = jnp.asarray(np.kron(np.eye(n_t * 2), copy_j), BF16)
    return pl.pallas_call(
        _expand_kernel,
        grid=(n_t,),
        in_specs=[
            pl.BlockSpec((1, ROW_TILE, n_in), lambda i: (i, 0, 0)),
            pl.BlockSpec((n_in, n_out), lambda i: (0, 0)),
        ],
        out_specs=pl.BlockSpec((1, ROW_TILE, n_out), lambda i: (i, 0, 0)),
        out_shape=jax.ShapeDtypeStruct((n_t, ROW_TILE, n_out), BF16),
        compiler_params=_params(("parallel",)),
        name="dft_stage2_matrix",
    )(compact, replicate)


def _trunk(x, score_bound, lnw, win, gq, gk, hmean, cs, m1, wf, bf, wout, fnorm):
    B, S, _ = x.shape
    assert S % ROW_TILE == 0
    s2 = S // DFT_A
    x4 = x.reshape(B, DFT_A, s2, D_MODEL)
    cos_t, sin_t = _rope_tables(S)
    qT, k, vT, gates, y = _proj_call(x4, lnw, win, gq, gk, cos_t, sin_t, hmean, cs, m1)
    ya = lax.cond(score_bound <= SCORE_BOUND_LOG2,
                  functools.partial(_attn_call, bounded=True),
                  functools.partial(_attn_call, bounded=False), qT, k, vT)
    g = _stage2_matrices(S)
    out = _out_call(x4, ya, gates, y, g, wf, bf, wout, fnorm)
    return out.reshape(B, S, D_MODEL)


def kernel(x_prompt, x_sample, ln_w, w_in, q_norm, k_norm, w_fourier, b_fourier, w_out, final_norm):
    assert ln_w.shape[0] == 1, "single mixer layer"
    lnw = ln_w[0].reshape(1, D_MODEL)
    win = w_in[0].astype(BF16)
    q_scale = HEAD_DIM ** -0.5 * math.log2(math.e)
    gq = jnp.tile(q_norm[0] * q_scale, N_Q_HEADS).reshape(1, ATTN_WIDTH)
    gk = jnp.tile(k_norm[0], N_KV_HEADS).reshape(1, KV_WIDTH)
    score_bound = HEAD_DIM * jnp.max(jnp.abs(q_norm[0] * q_scale)) * jnp.max(jnp.abs(k_norm[0]))
    head_id = np.arange(ATTN_WIDTH) // HEAD_DIM
    hmean = jnp.asarray((head_id[:, None] == head_id[None, :]) / HEAD_DIM, BF16)
    c_c, s_c = _dft_cos_sin(FOURIER_GROUP_DIM)
    cs = jnp.asarray(np.block([[c_c, -s_c], [-s_c, -c_c]]).astype(np.float32)).astype(BF16)
    m1 = _stage1_matrix()
    wf = w_fourier[0].astype(BF16)
    bf = b_fourier[0].reshape(1, FOURIER_WIDTH)
    wout = w_out[0].astype(BF16)
    fnorm = final_norm.reshape(1, D_MODEL)
    args = (lnw, win, gq, gk, hmean, cs, m1, wf, bf, wout, fnorm)
    return (_trunk(x_prompt, score_bound, *args), _trunk(x_sample, score_bound, *args))
```

```python
import functools
import math

import jax
import jax.numpy as jnp
import numpy as np
from jax import lax
from jax.experimental import pallas as pl
from jax.experimental.pallas import tpu as pltpu

D_MODEL = 1024
GRID_W = 64
ATTN_WIDTH = 512
FOURIER_WIDTH = 512
HEAD_DIM = 64
N_Q_HEADS = 8
N_KV_HEADS = 2
GQA_GROUP = 4
KV_WIDTH = 128
N_FOURIER_GROUPS = 4
FOURIER_GROUP_DIM = 128
ROPE_THETA = 10000.0
EPS = 1e-6
IN_WIDTH = 2304
Q0, K0, V0, GA0, U0, GF0 = 0, 512, 640, 768, 1280, 1792

VT_ROWS = HEAD_DIM + 16

DFT_A = 64
SUBLANES = 8
TILE_J = SUBLANES
ROW_TILE = DFT_A * TILE_J
Q_TILE = 512
VMEM_LIMIT_BYTES = 48 * 1024 * 1024
SCORE_BOUND_LOG2 = 64.0
BOUNDED_CHUNKS_PER_TRIP = 16

F32 = jnp.float32
BF16 = jnp.bfloat16


def _params(semantics):
    return pltpu.CompilerParams(dimension_semantics=semantics, vmem_limit_bytes=VMEM_LIMIT_BYTES)


def _rope(x, cos, sin_signed):
    n = x.shape[-1]
    lane_is_even = (lax.broadcasted_iota(jnp.int32, x.shape, 1) & 1) == 0
    partner = jnp.where(lane_is_even, pltpu.roll(x, n - 1, 1), pltpu.roll(x, 1, 1))
    return x * cos + partner * sin_signed


def _proj_kernel(x_ref, lnw_ref, win_ref, gq_ref, gk_ref, cos_ref, sin_ref, hmean_ref, cs_ref, m1_ref,
                 qT_ref, k_ref, vT_ref, gate_ref, y_ref):
    tm = ROW_TILE
    x = x_ref[0].reshape(tm, D_MODEL)
    ms = jnp.mean(x * x, axis=-1, keepdims=True)
    h = (x * lax.rsqrt(ms + EPS) * lnw_ref[...]).astype(BF16)
    proj = jnp.dot(h, win_ref[...], preferred_element_type=F32)

    cos2 = cos_ref[...]
    sin2 = sin_ref[...]
    cos = jnp.concatenate([cos2] * (ATTN_WIDTH // KV_WIDTH), axis=1)
    sin = jnp.concatenate([sin2] * (ATTN_WIDTH // KV_WIDTH), axis=1)
    hmean = hmean_ref[...]

    q = proj[:, Q0:Q0 + ATTN_WIDTH]
    q_ms = jnp.dot((q * q).astype(BF16), hmean, preferred_element_type=F32)
    q = q * lax.rsqrt(q_ms + EPS) * gq_ref[...]
    q = _rope(q, cos, sin)
    qT = q.T.astype(BF16)
    for hd in range(N_Q_HEADS):
        j, hh = divmod(hd, GQA_GROUP)
        for t in range(tm // Q_TILE):
            qT_ref[0, j, t, :, hh * Q_TILE:(hh + 1) * Q_TILE] = (
                qT[hd * HEAD_DIM:(hd + 1) * HEAD_DIM, t * Q_TILE:(t + 1) * Q_TILE])

    k = proj[:, K0:K0 + KV_WIDTH]
    k_ms = jnp.dot((k * k).astype(BF16), hmean[:KV_WIDTH, :KV_WIDTH], preferred_element_type=F32)
    k = k * lax.rsqrt(k_ms + EPS) * gk_ref[...]
    k = _rope(k, cos2, sin2).astype(BF16)
    for j in range(N_KV_HEADS):
        k_ref[0, j] = k[:, j * HEAD_DIM:(j + 1) * HEAD_DIM]

    vT = proj[:, V0:V0 + KV_WIDTH].T.astype(BF16)
    ones = jnp.ones((VT_ROWS - HEAD_DIM, tm), BF16)
    for j in range(N_KV_HEADS):
        vT_ref[0, j, 0, :HEAD_DIM, :] = vT[j * HEAD_DIM:(j + 1) * HEAD_DIM]
        vT_ref[0, j, 0, HEAD_DIM:, :] = ones

    gate_ref[0, :, :ATTN_WIDTH] = jax.nn.silu(proj[:, GA0:GA0 + ATTN_WIDTH]).astype(BF16)
    gate_ref[0, :, ATTN_WIDTH:] = jax.nn.silu(proj[:, GF0:GF0 + FOURIER_WIDTH]).astype(BF16)

    u = proj[:, U0:U0 + FOURIER_WIDTH].astype(BF16)
    pq = jnp.dot(m1_ref[...], u, preferred_element_type=F32).astype(BF16)
    cs = cs_ref[...]
    for g in range(N_FOURIER_GROUPS):
        cols = slice(g * FOURIER_GROUP_DIM, (g + 1) * FOURIER_GROUP_DIM)
        y = jnp.dot(jnp.concatenate([pq[:tm, cols], pq[tm:, cols]], axis=1), cs, preferred_element_type=F32)
        y_ref[0, 0, 0, :, cols] = y[:, :FOURIER_GROUP_DIM].astype(BF16)
        y_ref[0, 0, 1, :, cols] = y[:, FOURIER_GROUP_DIM:].astype(BF16)


def _proj_call(x4, lnw, win, gq, gk, cos_t, sin_t, hmean, cs, m1):
    B, _, S2, _ = x4.shape
    S = DFT_A * S2
    tm = ROW_TILE
    n_t = S // tm
    const = lambda shape: pl.BlockSpec(shape, lambda b, i: (0,) * len(shape))
    return pl.pallas_call(
        _proj_kernel,
        grid=(B, n_t),
        in_specs=[
            pl.BlockSpec((1, DFT_A, TILE_J, D_MODEL), lambda b, i: (b, 0, i, 0)),
            const((1, D_MODEL)),
            const((D_MODEL, IN_WIDTH)),
            const((1, ATTN_WIDTH)),
            const((1, KV_WIDTH)),
            pl.BlockSpec((tm, KV_WIDTH), lambda b, i: (i, 0)),
            pl.BlockSpec((tm, KV_WIDTH), lambda b, i: (i, 0)),
            const((ATTN_WIDTH, ATTN_WIDTH)),
            const((2 * FOURIER_GROUP_DIM, 2 * FOURIER_GROUP_DIM)),
            const((2 * tm, tm)),
        ],
        out_specs=[
            pl.BlockSpec((1, N_KV_HEADS, tm // Q_TILE, HEAD_DIM, GQA_GROUP * Q_TILE), lambda b, i: (b, 0, i, 0, 0)),
            pl.BlockSpec((1, N_KV_HEADS, tm, HEAD_DIM), lambda b, i: (b, 0, i, 0)),
            pl.BlockSpec((1, N_KV_HEADS, 1, VT_ROWS, tm), lambda b, i: (b, 0, i, 0, 0)),
            pl.BlockSpec((1, tm, 2 * ATTN_WIDTH), lambda b, i: (b, i, 0)),
            pl.BlockSpec((1, 1, 2, tm, FOURIER_WIDTH), lambda b, i: (b, i, 0, 0, 0)),
        ],
        out_shape=[
            jax.ShapeDtypeStruct((B, N_KV_HEADS, S // Q_TILE, HEAD_DIM, GQA_GROUP * Q_TILE), BF16),
            jax.ShapeDtypeStruct((B, N_KV_HEADS, S, HEAD_DIM), BF16),
            jax.ShapeDtypeStruct((B, N_KV_HEADS, n_t, VT_ROWS, tm), BF16),
            jax.ShapeDtypeStruct((B, S, 2 * ATTN_WIDTH), BF16),
            jax.ShapeDtypeStruct((B, n_t, 2, tm, FOURIER_WIDTH), BF16),
        ],
        compiler_params=_params(("parallel", "parallel")),
        name="proj",
    )(x4, lnw, win, gq, gk, cos_t, sin_t, hmean, cs, m1)


def _attn_kernel(qT_ref, k_ref, vT_ref, o_ref, m_ref, acc_ref, s_ref, cmax_ref, *, n_chunks, tk, unroll):
    n_q = qT_ref.shape[2]
    tq = qT_ref.shape[4] // GQA_GROUP

    def scores(t, c, slot):
        off = pl.multiple_of(c * tk, tk)
        s = jnp.dot(k_ref[0, 0, pl.ds(off, tk), :], qT_ref[0, 0, t], preferred_element_type=F32)
        s_ref[slot] = s
        cmax_ref[slot, 0:1] = jnp.max(s, axis=0, keepdims=True)

    def accumulate(c, slot):
        m_old = m_ref[0:1]
        m_new = jnp.maximum(m_old, cmax_ref[slot, 0:1])
        p = jnp.exp2(s_ref[slot] - m_new).astype(BF16)
        alpha = jnp.exp2(m_old - m_new)
        pv = jnp.dot(vT_ref[0, 0, c], p, preferred_element_type=F32)
        acc_ref[...] = alpha * acc_ref[...] + pv
        m_ref[0:1] = m_new

    scores(0, 0, 0)

    def tile(t, carry):
        m_ref[0:1] = jnp.full((1, m_ref.shape[1]), -jnp.inf, F32)
        acc_ref[...] = jnp.zeros(acc_ref.shape, F32)

        def group(i, inner):
            c = unroll * i
            for u in range(unroll):
                scores(t, c + u + 1, (u + 1) % 2)
                accumulate(c + u, u % 2)
            return inner

        lax.fori_loop(0, n_chunks // unroll - 1, group, 0)
        c = n_chunks - unroll
        for u in range(unroll):
            if u < unroll - 1:
                scores(t, c + u + 1, (u + 1) % 2)
            else:
                scores(jnp.minimum(t + 1, n_q - 1), 0, 0)
            accumulate(c + u, u % 2)
        acc = acc_ref[...]
        oT = acc[:HEAD_DIM] / acc[HEAD_DIM:HEAD_DIM + 1]
        oT = jnp.concatenate([oT[:, hh * tq:(hh + 1) * tq] for hh in range(GQA_GROUP)], axis=0)
        o_ref[0, pl.ds(pl.multiple_of(t * tq, tq), tq), :] = oT.T.astype(BF16)
        return carry

    lax.fori_loop(0, n_q, tile, 0)


def _attn_bounded_kernel(qT_ref, k_ref, vT_ref, o_ref, *, n_chunks, tk):
    n_q = qT_ref.shape[2]
    tq = qT_ref.shape[4] // GQA_GROUP

    def tile(t, carry):
        acc = None
        for c in range(n_chunks):
            s = jnp.dot(k_ref[0, 0, c * tk:(c + 1) * tk, :], qT_ref[0, 0, t], preferred_element_type=F32)
            pv = jnp.dot(vT_ref[0, 0, c], jnp.exp2(s).astype(BF16), preferred_element_type=F32)
            acc = pv if acc is None else acc + pv
        oT = acc[:HEAD_DIM] / acc[HEAD_DIM:HEAD_DIM + 1]
        oT = jnp.concatenate([oT[:, hh * tq:(hh + 1) * tq] for hh in range(GQA_GROUP)], axis=0)
        o_ref[0, pl.ds(pl.multiple_of(t * tq, tq), tq), :] = oT.T.astype(BF16)
        return carry

    tiles_per_trip = max(1, BOUNDED_CHUNKS_PER_TRIP // n_chunks)
    assert n_q % tiles_per_trip == 0

    def trip(i, carry):
        for u in range(tiles_per_trip):
            tile(tiles_per_trip * i + u, carry)
        return carry

    lax.fori_loop(0, n_q // tiles_per_trip, trip, 0)


def _attn_call(qT, k, vT, bounded):
    B, _, n_q, _, _ = qT.shape
    S = k.shape[2]
    n_chunks, tk = vT.shape[2], vT.shape[4]
    unroll = 4 if n_chunks >= 16 else 2
    assert n_chunks % unroll == 0 and n_chunks >= 2 * unroll
    tq = Q_TILE
    width = GQA_GROUP * HEAD_DIM
    if bounded:
        body = functools.partial(_attn_bounded_kernel, n_chunks=n_chunks, tk=tk)
        scratch = []
    else:
        body = functools.partial(_attn_kernel, n_chunks=n_chunks, tk=tk, unroll=unroll)
        scratch = [pltpu.VMEM((SUBLANES, GQA_GROUP * tq), F32), pltpu.VMEM((VT_ROWS, GQA_GROUP * tq), F32),
                   pltpu.VMEM((2, tk, GQA_GROUP * tq), F32), pltpu.VMEM((2, SUBLANES, GQA_GROUP * tq), F32)]
    return pl.pallas_call(
        body,
        grid=(B, N_KV_HEADS),
        in_specs=[
            pl.BlockSpec((1, 1, n_q, HEAD_DIM, GQA_GROUP * tq), lambda b, j: (b, j, 0, 0, 0)),
            pl.BlockSpec((1, 1, S, HEAD_DIM), lambda b, j: (b, j, 0, 0)),
            pl.BlockSpec((1, 1, n_chunks, VT_ROWS, tk), lambda b, j: (b, j, 0, 0, 0)),
        ],
        out_specs=pl.BlockSpec((1, S, width), lambda b, j: (b, 0, j)),
        out_shape=jax.ShapeDtypeStruct((B, S, ATTN_WIDTH), BF16),
        scratch_shapes=scratch,
        compiler_params=_params(("parallel", "parallel")),
        name="attn_bounded" if bounded else "attn",
    )(qT, k, vT)


def _out_kernel(x_ref, ya_ref, gate_ref, y_ref, g_ref, wf_ref, bf_ref, wout_ref, fn_ref, o_ref, *, scale):
    tm = ROW_TILE
    gate_a = gate_ref[0, :, :ATTN_WIDTH].astype(F32)
    gate_f = gate_ref[0, :, ATTN_WIDTH:].astype(F32)
    ya = (ya_ref[0].astype(F32) * gate_a).astype(BF16)
    y = y_ref[0].reshape(g_ref.shape[2], FOURIER_WIDTH)
    mix = (jnp.dot(g_ref[0], y, preferred_element_type=F32) * scale).astype(BF16)
    parts = []
    for g in range(N_FOURIER_GROUPS):
        lo = g * FOURIER_GROUP_DIM
        parts.append(jnp.dot(mix[:, lo:lo + FOURIER_GROUP_DIM], wf_ref[g], preferred_element_type=F32))
    yf = ((jnp.concatenate(parts, axis=-1) + bf_ref[...]) * gate_f).astype(BF16)
    out = x_ref[0].reshape(tm, D_MODEL)
    out = out + jnp.dot(ya, wout_ref[:ATTN_WIDTH, :], preferred_element_type=F32)
    out = out + jnp.dot(yf, wout_ref[ATTN_WIDTH:, :], preferred_element_type=F32)
    ms = jnp.mean(out * out, axis=-1, keepdims=True)
    o_ref[0] = (out * lax.rsqrt(ms + EPS) * fn_ref[...]).reshape(DFT_A, TILE_J, D_MODEL)


def _out_call(x4, ya, gates, y, g, wf, bf, wout, fnorm):
    B, _, S2, _ = x4.shape
    S = DFT_A * S2
    tm = ROW_TILE
    n_t = S // tm
    k1_rows = TILE_J * TILE_J
    n_k1_blocks = DFT_A // TILE_J
    const = lambda shape: pl.BlockSpec(shape, lambda i, b: (0,) * len(shape))
    scale = 1.0 / math.sqrt(S * FOURIER_GROUP_DIM)
    return pl.pallas_call(
        functools.partial(_out_kernel, scale=scale),
        grid=(n_t, B),
        in_specs=[
            pl.BlockSpec((1, DFT_A, TILE_J, D_MODEL), lambda i, b: (b, 0, i, 0)),
            pl.BlockSpec((1, tm, ATTN_WIDTH), lambda i, b: (b, i, 0)),
            pl.BlockSpec((1, tm, 2 * ATTN_WIDTH), lambda i, b: (b, i, 0)),
            pl.BlockSpec((1, n_t, 2, k1_rows, FOURIER_WIDTH), lambda i, b: (b, 0, 0, i % n_k1_blocks, 0)),
            pl.BlockSpec((1, tm, n_t * 2 * k1_rows), lambda i, b: (i, 0, 0)),
            const((N_FOURIER_GROUPS, FOURIER_GROUP_DIM, FOURIER_GROUP_DIM)),
            const((1, FOURIER_WIDTH)),
            const((D_MODEL, D_MODEL)),
            const((1, D_MODEL)),
        ],
        out_specs=pl.BlockSpec((1, DFT_A, TILE_J, D_MODEL), lambda i, b: (b, 0, i, 0)),
        out_shape=jax.ShapeDtypeStruct((B, DFT_A, S2, D_MODEL), F32),
        compiler_params=_params(("parallel", "parallel")),
        name="out_proj",
    )(x4, ya, gates, y, g, wf, bf, wout, fnorm)


def _tile_positions(seq_len):
    s2 = seq_len // DFT_A
    i = np.arange(s2 // TILE_J)[:, None, None]
    a = np.arange(DFT_A)[None, :, None]
    j = np.arange(TILE_J)[None, None, :]
    return (a * s2 + TILE_J * i + j).reshape(-1)


def _rope_tables(seq_len):
    pos = _tile_positions(seq_len)
    axis_dim = HEAD_DIM // 2
    inv_freq = ROPE_THETA ** (-np.arange(0, axis_dim, 2, dtype=np.float64) / axis_dim)
    row = (pos // GRID_W).astype(np.float64)
    col = (pos % GRID_W).astype(np.float64)
    ang = np.concatenate([row[:, None] * inv_freq, col[:, None] * inv_freq], axis=-1)
    cos, sin = np.cos(ang), np.sin(ang)
    cos_pair = np.repeat(cos, 2, axis=-1)
    sin_pair = np.stack([-sin, sin], axis=-1).reshape(seq_len, HEAD_DIM)
    tables = [np.tile(t, (1, N_KV_HEADS)).astype(np.float32) for t in (cos_pair, sin_pair)]
    return jnp.asarray(tables[0]), jnp.asarray(tables[1])


def _dft_cos_sin(n):
    idx = np.arange(n, dtype=np.int64)
    ang = ((idx[:, None] * idx[None, :]) % n).astype(np.float64) * (2.0 * math.pi / n)
    return np.cos(ang), np.sin(ang)


def _stage1_matrix():
    c, s = _dft_cos_sin(DFT_A)
    eye = np.eye(TILE_J)
    stacked = np.concatenate([np.kron(c, eye), np.kron(s, eye)], axis=0)
    return jnp.asarray(stacked.astype(np.float32)).astype(BF16)


def _expand_kernel(t_ref, r_ref, g_ref):
    g = jnp.dot(t_ref[0], r_ref[...], preferred_element_type=F32)
    row_j = lax.broadcasted_iota(jnp.int32, g.shape, 0) % TILE_J
    col_jj = (lax.broadcasted_iota(jnp.int32, g.shape, 1) // TILE_J) % TILE_J
    g_ref[0] = jnp.where(row_j == col_jj, g, 0.0).astype(BF16)


def _stage2_matrices(seq_len):
    s2_len = seq_len // DFT_A
    n_t = s2_len // TILE_J
    k = _tile_positions(seq_len).astype(np.int64)
    s2 = np.arange(s2_len, dtype=np.int64)
    ang = ((k[:, None] * s2[None, :]) % seq_len).astype(np.float64) * (2.0 * math.pi / seq_len)
    trig = np.stack([np.cos(ang), np.sin(ang)], axis=1)
    trig = trig.reshape(n_t, ROW_TILE, 2, n_t, TILE_J).transpose(0, 1, 3, 2, 4)
    n_in = n_t * 2 * TILE_J
    n_out = n_in * TILE_J
    compact = jnp.asarray(trig.reshape(n_t, ROW_TILE, n_in).astype(np.float32)).astype(BF16)
    copy_j = np.kron(np.ones((1, TILE_J)), np.eye(TILE_J))
    replicate = jnp.asarray(np.kron(np.eye(n_t * 2), copy_j), BF16)
    return pl.pallas_call(
        _expand_kernel,
        grid=(n_t,),
        in_specs=[
            pl.BlockSpec((1, ROW_TILE, n_in), lambda i: (i, 0, 0)),
            pl.BlockSpec((n_in, n_out), lambda i: (0, 0)),
        ],
        out_specs=pl.BlockSpec((1, ROW_TILE, n_out), lambda i: (i, 0, 0)),
        out_shape=jax.ShapeDtypeStruct((n_t, ROW_TILE, n_out), BF16),
        compiler_params=_params(("parallel",)),
        name="dft_stage2_matrix",
    )(compact, replicate)


def _trunk(x, score_bound, lnw, win, gq, gk, hmean, cs, m1, wf, bf, wout, fnorm):
    B, S, _ = x.shape
    assert S % ROW_TILE == 0
    s2 = S // DFT_A
    x4 = x.reshape(B, DFT_A, s2, D_MODEL)
    cos_t, sin_t = _rope_tables(S)
    qT, k, vT, gates, y = _proj_call(x4, lnw, win, gq, gk, cos_t, sin_t, hmean, cs, m1)
    ya = lax.cond(score_bound <= SCORE_BOUND_LOG2,
                  functools.partial(_attn_call, bounded=True),
                  functools.partial(_attn_call, bounded=False), qT, k, vT)
    g = _stage2_matrices(S)
    out = _out_call(x4, ya, gates, y, g, wf, bf, wout, fnorm)
    return out.reshape(B, S, D_MODEL)


def kernel(x_prompt, x_sample, ln_w, w_in, q_norm, k_norm, w_fourier, b_fourier, w_out, final_norm):
    assert ln_w.shape[0] == 1, "single mixer layer"
    lnw = ln_w[0].reshape(1, D_MODEL)
    win = w_in[0].astype(BF16)
    q_scale = HEAD_DIM ** -0.5 * math.log2(math.e)
    gq = jnp.tile(q_norm[0] * q_scale, N_Q_HEADS).reshape(1, ATTN_WIDTH)
    gk = jnp.tile(k_norm[0], N_KV_HEADS).reshape(1, KV_WIDTH)
    score_bound = HEAD_DIM * jnp.max(jnp.abs(q_norm[0] * q_scale)) * jnp.max(jnp.abs(k_norm[0]))
    head_id = np.arange(ATTN_WIDTH) // HEAD_DIM
    hmean = jnp.asarray((head_id[:, None] == head_id[None, :]) / HEAD_DIM, BF16)
    c_c, s_c = _dft_cos_sin(FOURIER_GROUP_DIM)
    cs = jnp.asarray(np.block([[c_c, -s_c], [-s_c, -c_c]]).astype(np.float32)).astype(BF16)
    m1 = _stage1_matrix()
    wf = w_fourier[0].astype(BF16)
    bf = b_fourier[0].reshape(1, FOURIER_WIDTH)
    wout = w_out[0].astype(BF16)
    fnorm = final_norm.reshape(1, D_MODEL)
    args = (lnw, win, gq, gk, hmean, cs, m1, wf, bf, wout, fnorm)
    return (_trunk(x_prompt, score_bound, *args), _trunk(x_sample, score_bound, *args))
```

```python
import functools
import math

import jax
import jax.numpy as jnp
import numpy as np
from jax import lax
from jax.experimental import pallas as pl
from jax.experimental.pallas import tpu as pltpu

D_MODEL = 1024
GRID_W = 64
ATTN_WIDTH = 512
FOURIER_WIDTH = 512
HEAD_DIM = 64
N_Q_HEADS = 8
N_KV_HEADS = 2
GQA_GROUP = 4
KV_WIDTH = 128
N_FOURIER_GROUPS = 4
FOURIER_GROUP_DIM = 128
ROPE_THETA = 10000.0
EPS = 1e-6
IN_WIDTH = 2304
Q0, K0, V0, GA0, U0, GF0 = 0, 512, 640, 768, 1280, 1792

VT_ROWS = HEAD_DIM + 16

DFT_A = 64
SUBLANES = 8
TILE_J = SUBLANES
ROW_TILE = DFT_A * TILE_J
Q_TILE = 512
VMEM_LIMIT_BYTES = 48 * 1024 * 1024
SCORE_BOUND_LOG2 = 64.0
BOUNDED_CHUNKS_PER_TRIP = 16

F32 = jnp.float32
BF16 = jnp.bfloat16


def _params(semantics):
    return pltpu.CompilerParams(dimension_semantics=semantics, vmem_limit_bytes=VMEM_LIMIT_BYTES)


def _rope(x, cos, sin_signed):
    n = x.shape[-1]
    lane_is_even = (lax.broadcasted_iota(jnp.int32, x.shape, 1) & 1) == 0
    partner = jnp.where(lane_is_even, pltpu.roll(x, n - 1, 1), pltpu.roll(x, 1, 1))
    return x * cos + partner * sin_signed


def _proj_kernel(x_ref, lnw_ref, win_ref, gq_ref, gk_ref, cos_ref, sin_ref, hmean_ref, cs_ref, m1_ref,
                 qT_ref, k_ref, vT_ref, gate_ref, y_ref):
    tm = ROW_TILE
    x = x_ref[0].reshape(tm, D_MODEL)
    ms = jnp.mean(x * x, axis=-1, keepdims=True)
    h = (x * lax.rsqrt(ms + EPS) * lnw_ref[...]).astype(BF16)
    proj = jnp.dot(h, win_ref[...], preferred_element_type=F32)

    cos2 = cos_ref[...]
    sin2 = sin_ref[...]
    cos = jnp.concatenate([cos2] * (ATTN_WIDTH // KV_WIDTH), axis=1)
    sin = jnp.concatenate([sin2] * (ATTN_WIDTH // KV_WIDTH), axis=1)
    hmean = hmean_ref[...]

    q = proj[:, Q0:Q0 + ATTN_WIDTH]
    q_ms = jnp.dot((q * q).astype(BF16), hmean, preferred_element_type=F32)
    q = q * lax.rsqrt(q_ms + EPS) * gq_ref[...]
    q = _rope(q, cos, sin)
    qT = q.T.astype(BF16)
    for hd in range(N_Q_HEADS):
        j, hh = divmod(hd, GQA_GROUP)
        for t in range(tm // Q_TILE):
            qT_ref[0, j, t, :, hh * Q_TILE:(hh + 1) * Q_TILE] = (
                qT[hd * HEAD_DIM:(hd + 1) * HEAD_DIM, t * Q_TILE:(t + 1) * Q_TILE])

    k = proj[:, K0:K0 + KV_WIDTH]
    k_ms = jnp.dot((k * k).astype(BF16), hmean[:KV_WIDTH, :KV_WIDTH], preferred_element_type=F32)
    k = k * lax.rsqrt(k_ms + EPS) * gk_ref[...]
    k = _rope(k, cos2, sin2).astype(BF16)
    for j in range(N_KV_HEADS):
        k_ref[0, j] = k[:, j * HEAD_DIM:(j + 1) * HEAD_DIM]

    vT = proj[:, V0:V0 + KV_WIDTH].T.astype(BF16)
    ones = jnp.ones((VT_ROWS - HEAD_DIM, tm), BF16)
    for j in range(N_KV_HEADS):
        vT_ref[0, j, 0, :HEAD_DIM, :] = vT[j * HEAD_DIM:(j + 1) * HEAD_DIM]
        vT_ref[0, j, 0, HEAD_DIM:, :] = ones

    gate_ref[0, :, :ATTN_WIDTH] = jax.nn.silu(proj[:, GA0:GA0 + ATTN_WIDTH]).astype(BF16)
    gate_ref[0, :, ATTN_WIDTH:] = jax.nn.silu(proj[:, GF0:GF0 + FOURIER_WIDTH]).astype(BF16)

    u = proj[:, U0:U0 + FOURIER_WIDTH].astype(BF16)
    pq = jnp.dot(m1_ref[...], u, preferred_element_type=F32).astype(BF16)
    cs = cs_ref[...]
    for g in range(N_FOURIER_GROUPS):
        cols = slice(g * FOURIER_GROUP_DIM, (g + 1) * FOURIER_GROUP_DIM)
        y = jnp.dot(jnp.concatenate([pq[:tm, cols], pq[tm:, cols]], axis=1), cs, preferred_element_type=F32)
        y_ref[0, 0, 0, :, cols] = y[:, :FOURIER_GROUP_DIM].astype(BF16)
        y_ref[0, 0, 1, :, cols] = y[:, FOURIER_GROUP_DIM:].astype(BF16)


def _proj_call(x4, lnw, win, gq, gk, cos_t, sin_t, hmean, cs, m1):
    B, _, S2, _ = x4.shape
    S = DFT_A * S2
    tm = ROW_TILE
    n_t = S // tm
    const = lambda shape: pl.BlockSpec(shape, lambda b, i: (0,) * len(shape))
    return pl.pallas_call(
        _proj_kernel,
        grid=(B, n_t),
        in_specs=[
            pl.BlockSpec((1, DFT_A, TILE_J, D_MODEL), lambda b, i: (b, 0, i, 0)),
            const((1, D_MODEL)),
            const((D_MODEL, IN_WIDTH)),
            const((1, ATTN_WIDTH)),
            const((1, KV_WIDTH)),
            pl.BlockSpec((tm, KV_WIDTH), lambda b, i: (i, 0)),
            pl.BlockSpec((tm, KV_WIDTH), lambda b, i: (i, 0)),
            const((ATTN_WIDTH, ATTN_WIDTH)),
            const((2 * FOURIER_GROUP_DIM, 2 * FOURIER_GROUP_DIM)),
            const((2 * tm, tm)),
        ],
        out_specs=[
            pl.BlockSpec((1, N_KV_HEADS, tm // Q_TILE, HEAD_DIM, GQA_GROUP * Q_TILE), lambda b, i: (b, 0, i, 0, 0)),
            pl.BlockSpec((1, N_KV_HEADS, tm, HEAD_DIM), lambda b, i: (b, 0, i, 0)),
            pl.BlockSpec((1, N_KV_HEADS, 1, VT_ROWS, tm), lambda b, i: (b, 0, i, 0, 0)),
            pl.BlockSpec((1, tm, 2 * ATTN_WIDTH), lambda b, i: (b, i, 0)),
            pl.BlockSpec((1, 1, 2, tm, FOURIER_WIDTH), lambda b, i: (b, i, 0, 0, 0)),
        ],
        out_shape=[
            jax.ShapeDtypeStruct((B, N_KV_HEADS, S // Q_TILE, HEAD_DIM, GQA_GROUP * Q_TILE), BF16),
            jax.ShapeDtypeStruct((B, N_KV_HEADS, S, HEAD_DIM), BF16),
            jax.ShapeDtypeStruct((B, N_KV_HEADS, n_t, VT_ROWS, tm), BF16),
            jax.ShapeDtypeStruct((B, S, 2 * ATTN_WIDTH), BF16),
            jax.ShapeDtypeStruct((B, n_t, 2, tm, FOURIER_WIDTH), BF16),
        ],
        compiler_params=_params(("parallel", "parallel")),
        name="proj",
    )(x4, lnw, win, gq, gk, cos_t, sin_t, hmean, cs, m1)


def _attn_kernel(qT_ref, k_ref, vT_ref, o_ref, m_ref, acc_ref, s_ref, cmax_ref, *, n_chunks, tk, unroll):
    n_q = qT_ref.shape[2]
    tq = qT_ref.shape[4] // GQA_GROUP

    def scores(t, c, slot):
        off = pl.multiple_of(c * tk, tk)
        s = jnp.dot(k_ref[0, 0, pl.ds(off, tk), :], qT_ref[0, 0, t], preferred_element_type=F32)
        s_ref[slot] = s
        cmax_ref[slot, 0:1] = jnp.max(s, axis=0, keepdims=True)

    def accumulate(c, slot):
        m_old = m_ref[0:1]
        m_new = jnp.maximum(m_old, cmax_ref[slot, 0:1])
        p = jnp.exp2(s_ref[slot] - m_new).astype(BF16)
        alpha = jnp.exp2(m_old - m_new)
        pv = jnp.dot(vT_ref[0, 0, c], p, preferred_element_type=F32)
        acc_ref[...] = alpha * acc_ref[...] + pv
        m_ref[0:1] = m_new

    scores(0, 0, 0)

    def tile(t, carry):
        m_ref[0:1] = jnp.full((1, m_ref.shape[1]), -jnp.inf, F32)
        acc_ref[...] = jnp.zeros(acc_ref.shape, F32)

        def group(i, inner):
            c = unroll * i
            for u in range(unroll):
                scores(t, c + u + 1, (u + 1) % 2)
                accumulate(c + u, u % 2)
            return inner

        lax.fori_loop(0, n_chunks // unroll - 1, group, 0)
        c = n_chunks - unroll
        for u in range(unroll):
            if u < unroll - 1:
                scores(t, c + u + 1, (u + 1) % 2)
            else:
                scores(jnp.minimum(t + 1, n_q - 1), 0, 0)
            accumulate(c + u, u % 2)
        acc = acc_ref[...]
        oT = acc[:HEAD_DIM] / acc[HEAD_DIM:HEAD_DIM + 1]
        oT = jnp.concatenate([oT[:, hh * tq:(hh + 1) * tq] for hh in range(GQA_GROUP)], axis=0)
        o_ref[0, pl.ds(pl.multiple_of(t * tq, tq), tq), :] = oT.T.astype(BF16)
        return carry

    lax.fori_loop(0, n_q, tile, 0)


def _attn_bounded_kernel(qT_ref, k_ref, vT_ref, o_ref, *, n_chunks, tk):
    n_q = qT_ref.shape[2]
    tq = qT_ref.shape[4] // GQA_GROUP

    def tile(t, carry):
        acc = None
        for c in range(n_chunks):
            s = jnp.dot(k_ref[0, 0, c * tk:(c + 1) * tk, :], qT_ref[0, 0, t], preferred_element_type=F32)
            pv = jnp.dot(vT_ref[0, 0, c], jnp.exp2(s).astype(BF16), preferred_element_type=F32)
            acc = pv if acc is None else acc + pv
        oT = acc[:HEAD_DIM] / acc[HEAD_DIM:HEAD_DIM + 1]
        oT = jnp.concatenate([oT[:, hh * tq:(hh + 1) * tq] for hh in range(GQA_GROUP)], axis=0)
        o_ref[0, pl.ds(pl.multiple_of(t * tq, tq), tq), :] = oT.T.astype(BF16)
        return carry

    tiles_per_trip = max(1, BOUNDED_CHUNKS_PER_TRIP // n_chunks)
    assert n_q % tiles_per_trip == 0

    def trip(i, carry):
        for u in range(tiles_per_trip):
            tile(tiles_per_trip * i + u, carry)
        return carry

    lax.fori_loop(0, n_q // tiles_per_trip, trip, 0)


def _attn_call(qT, k, vT, bounded):
    B, _, n_q, _, _ = qT.shape
    S = k.shape[2]
    n_chunks, tk = vT.shape[2], vT.shape[4]
    unroll = 4 if n_chunks >= 16 else 2
    assert n_chunks % unroll == 0 and n_chunks >= 2 * unroll
    tq = Q_TILE
    width = GQA_GROUP * HEAD_DIM
    if bounded:
        body = functools.partial(_attn_bounded_kernel, n_chunks=n_chunks, tk=tk)
        scratch = []
    else:
        body = functools.partial(_attn_kernel, n_chunks=n_chunks, tk=tk, unroll=unroll)
        scratch = [pltpu.VMEM((SUBLANES, GQA_GROUP * tq), F32), pltpu.VMEM((VT_ROWS, GQA_GROUP * tq), F32),
                   pltpu.VMEM((2, tk, GQA_GROUP * tq), F32), pltpu.VMEM((2, SUBLANES, GQA_GROUP * tq), F32)]
    return pl.pallas_call(
        body,
        grid=(B, N_KV_HEADS),
        in_specs=[
            pl.BlockSpec((1, 1, n_q, HEAD_DIM, GQA_GROUP * tq), lambda b, j: (b, j, 0, 0, 0)),
            pl.BlockSpec((1, 1, S, HEAD_DIM), lambda b, j: (b, j, 0, 0)),
            pl.BlockSpec((1, 1, n_chunks, VT_ROWS, tk), lambda b, j: (b, j, 0, 0, 0)),
        ],
        out_specs=pl.BlockSpec((1, S, width), lambda b, j: (b, 0, j)),
        out_shape=jax.ShapeDtypeStruct((B, S, ATTN_WIDTH), BF16),
        scratch_shapes=scratch,
        compiler_params=_params(("parallel", "parallel")),
        name="attn_bounded" if bounded else "attn",
    )(qT, k, vT)


def _out_kernel(x_ref, ya_ref, gate_ref, y_ref, t_ref, r_ref, wf_ref, bf_ref, wout_ref, fn_ref, o_ref, *, scale):
    tm = ROW_TILE
    g = jnp.dot(t_ref[0], r_ref[...], preferred_element_type=F32)
    row_j = lax.broadcasted_iota(jnp.int32, g.shape, 0) % TILE_J
    col_jj = (lax.broadcasted_iota(jnp.int32, g.shape, 1) // TILE_J) % TILE_J
    g = jnp.where(row_j == col_jj, g, 0.0).astype(BF16)
    gate_a = gate_ref[0, :, :ATTN_WIDTH].astype(F32)
    gate_f = gate_ref[0, :, ATTN_WIDTH:].astype(F32)
    ya = (ya_ref[0].astype(F32) * gate_a).astype(BF16)
    y = y_ref[0].reshape(g.shape[1], FOURIER_WIDTH)
    mix = (jnp.dot(g, y, preferred_element_type=F32) * scale).astype(BF16)
    parts = []
    for g in range(N_FOURIER_GROUPS):
        lo = g * FOURIER_GROUP_DIM
        parts.append(jnp.dot(mix[:, lo:lo + FOURIER_GROUP_DIM], wf_ref[g], preferred_element_type=F32))
    yf = ((jnp.concatenate(parts, axis=-1) + bf_ref[...]) * gate_f).astype(BF16)
    out = x_ref[0].reshape(tm, D_MODEL)
    out = out + jnp.dot(ya, wout_ref[:ATTN_WIDTH, :], preferred_element_type=F32)
    out = out + jnp.dot(yf, wout_ref[ATTN_WIDTH:, :], preferred_element_type=F32)
    ms = jnp.mean(out * out, axis=-1, keepdims=True)
    o_ref[0] = (out * lax.rsqrt(ms + EPS) * fn_ref[...]).reshape(DFT_A, TILE_J, D_MODEL)


def _out_call(x4, ya, gates, y, twiddles, replicate, wf, bf, wout, fnorm):
    B, _, S2, _ = x4.shape
    S = DFT_A * S2
    tm = ROW_TILE
    n_t = S // tm
    k1_rows = TILE_J * TILE_J
    n_k1_blocks = DFT_A // TILE_J
    const = lambda shape: pl.BlockSpec(shape, lambda i, b: (0,) * len(shape))
    scale = 1.0 / math.sqrt(S * FOURIER_GROUP_DIM)
    return pl.pallas_call(
        functools.partial(_out_kernel, scale=scale),
        grid=(n_t, B),
        in_specs=[
            pl.BlockSpec((1, DFT_A, TILE_J, D_MODEL), lambda i, b: (b, 0, i, 0)),
            pl.BlockSpec((1, tm, ATTN_WIDTH), lambda i, b: (b, i, 0)),
            pl.BlockSpec((1, tm, 2 * ATTN_WIDTH), lambda i, b: (b, i, 0)),
            pl.BlockSpec((1, n_t, 2, k1_rows, FOURIER_WIDTH), lambda i, b: (b, 0, 0, i % n_k1_blocks, 0)),
            pl.BlockSpec((1, tm, twiddles.shape[2]), lambda i, b: (i, 0, 0)),
            const(replicate.shape),
            const((N_FOURIER_GROUPS, FOURIER_GROUP_DIM, FOURIER_GROUP_DIM)),
            const((1, FOURIER_WIDTH)),
            const((D_MODEL, D_MODEL)),
            const((1, D_MODEL)),
        ],
        out_specs=pl.BlockSpec((1, DFT_A, TILE_J, D_MODEL), lambda i, b: (b, 0, i, 0)),
        out_shape=jax.ShapeDtypeStruct((B, DFT_A, S2, D_MODEL), F32),
        compiler_params=_params(("parallel", "parallel")),
        name="out_proj",
    )(x4, ya, gates, y, twiddles, replicate, wf, bf, wout, fnorm)


def _tile_positions(seq_len):
    s2 = seq_len // DFT_A
    i = np.arange(s2 // TILE_J)[:, None, None]
    a = np.arange(DFT_A)[None, :, None]
    j = np.arange(TILE_J)[None, None, :]
    return (a * s2 + TILE_J * i + j).reshape(-1)


def _rope_tables(seq_len):
    pos = _tile_positions(seq_len)
    axis_dim = HEAD_DIM // 2
    inv_freq = ROPE_THETA ** (-np.arange(0, axis_dim, 2, dtype=np.float64) / axis_dim)
    row = (pos // GRID_W).astype(np.float64)
    col = (pos % GRID_W).astype(np.float64)
    ang = np.concatenate([row[:, None] * inv_freq, col[:, None] * inv_freq], axis=-1)
    cos, sin = np.cos(ang), np.sin(ang)
    cos_pair = np.repeat(cos, 2, axis=-1)
    sin_pair = np.stack([-sin, sin], axis=-1).reshape(seq_len, HEAD_DIM)
    tables = [np.tile(t, (1, N_KV_HEADS)).astype(np.float32) for t in (cos_pair, sin_pair)]
    return jnp.asarray(tables[0]), jnp.asarray(tables[1])


def _dft_cos_sin(n):
    idx = np.arange(n, dtype=np.int64)
    ang = ((idx[:, None] * idx[None, :]) % n).astype(np.float64) * (2.0 * math.pi / n)
    return np.cos(ang), np.sin(ang)


def _stage1_matrix():
    c, s = _dft_cos_sin(DFT_A)
    eye = np.eye(TILE_J)
    stacked = np.concatenate([np.kron(c, eye), np.kron(s, eye)], axis=0)
    return jnp.asarray(stacked.astype(np.float32)).astype(BF16)


def _stage2_twiddles(seq_len):
    s2_len = seq_len // DFT_A
    n_t = s2_len // TILE_J
    k = _tile_positions(seq_len).astype(np.int64)
    s2 = np.arange(s2_len, dtype=np.int64)
    ang = ((k[:, None] * s2[None, :]) % seq_len).astype(np.float64) * (2.0 * math.pi / seq_len)
    trig = np.stack([np.cos(ang), np.sin(ang)], axis=1)
    trig = trig.reshape(n_t, ROW_TILE, 2, n_t, TILE_J).transpose(0, 1, 3, 2, 4)
    n_in = n_t * 2 * TILE_J
    n_out = n_in * TILE_J
    compact = jnp.asarray(trig.reshape(n_t, ROW_TILE, n_in).astype(np.float32)).astype(BF16)
    copy_j = np.kron(np.ones((1, TILE_J)), np.eye(TILE_J))
    replicate = jnp.asarray(np.kron(np.eye(n_t * 2), copy_j), BF16)
    return compact, replicate


def _trunk(x, score_bound, lnw, win, gq, gk, hmean, cs, m1, wf, bf, wout, fnorm):
    B, S, _ = x.shape
    assert S % ROW_TILE == 0
    s2 = S // DFT_A
    x4 = x.reshape(B, DFT_A, s2, D_MODEL)
    cos_t, sin_t = _rope_tables(S)
    qT, k, vT, gates, y = _proj_call(x4, lnw, win, gq, gk, cos_t, sin_t, hmean, cs, m1)
    ya = lax.cond(score_bound <= SCORE_BOUND_LOG2,
                  functools.partial(_attn_call, bounded=True),
                  functools.partial(_attn_call, bounded=False), qT, k, vT)
    twiddles, replicate = _stage2_twiddles(S)
    out = _out_call(x4, ya, gates, y, twiddles, replicate, wf, bf, wout, fnorm)
    return out.reshape(B, S, D_MODEL)


def kernel(x_prompt, x_sample, ln_w, w_in, q_norm, k_norm, w_fourier, b_fourier, w_out, final_norm):
    assert ln_w.shape[0] == 1, "single mixer layer"
    lnw = ln_w[0].reshape(1, D_MODEL)
    win = w_in[0].astype(BF16)
    q_scale = HEAD_DIM ** -0.5 * math.log2(math.e)
    gq = jnp.tile(q_norm[0] * q_scale, N_Q_HEADS).reshape(1, ATTN_WIDTH)
    gk = jnp.tile(k_norm[0], N_KV_HEADS).reshape(1, KV_WIDTH)
    score_bound = HEAD_DIM * jnp.max(jnp.abs(q_norm[0] * q_scale)) * jnp.max(jnp.abs(k_norm[0]))
    head_id = np.arange(ATTN_WIDTH) // HEAD_DIM
    hmean = jnp.asarray((head_id[:, None] == head_id[None, :]) / HEAD_DIM, BF16)
    c_c, s_c = _dft_cos_sin(FOURIER_GROUP_DIM)
    cs = jnp.asarray(np.block([[c_c, -s_c], [-s_c, -c_c]]).astype(np.float32)).astype(BF16)
    m1 = _stage1_matrix()
    wf = w_fourier[0].astype(BF16)
    bf = b_fourier[0].reshape(1, FOURIER_WIDTH)
    wout = w_out[0].astype(BF16)
    fnorm = final_norm.reshape(1, D_MODEL)
    args = (lnw, win, gq, gk, hmean, cs, m1, wf, bf, wout, fnorm)
    return (_trunk(x_prompt, score_bound, *args), _trunk(x_sample, score_bound, *args))
```

```python
import functools
import math

import jax
import jax.numpy as jnp
import numpy as np
from jax import lax
from jax.experimental import pallas as pl
from jax.experimental.pallas import tpu as pltpu

D_MODEL = 1024
GRID_W = 64
ATTN_WIDTH = 512
FOURIER_WIDTH = 512
HEAD_DIM = 64
N_Q_HEADS = 8
N_KV_HEADS = 2
GQA_GROUP = 4
KV_WIDTH = 128
N_FOURIER_GROUPS = 4
FOURIER_GROUP_DIM = 128
ROPE_THETA = 10000.0
EPS = 1e-6
IN_WIDTH = 2304
Q0, K0, V0, GA0, U0, GF0 = 0, 512, 640, 768, 1280, 1792

VT_ROWS = HEAD_DIM + 16

DFT_A = 64
SUBLANES = 8
TILE_J = SUBLANES
ROW_TILE = DFT_A * TILE_J
Q_TILE = 512
OUT_BATCH = 2
VMEM_LIMIT_BYTES = 48 * 1024 * 1024
SCORE_BOUND_LOG2 = 64.0
BOUNDED_CHUNKS_PER_TRIP = 16

F32 = jnp.float32
BF16 = jnp.bfloat16


def _params(semantics):
    return pltpu.CompilerParams(dimension_semantics=semantics, vmem_limit_bytes=VMEM_LIMIT_BYTES)


def _rope(x, cos, sin_signed):
    n = x.shape[-1]
    lane_is_even = (lax.broadcasted_iota(jnp.int32, x.shape, 1) & 1) == 0
    partner = jnp.where(lane_is_even, pltpu.roll(x, n - 1, 1), pltpu.roll(x, 1, 1))
    return x * cos + partner * sin_signed


def _proj_kernel(x_ref, lnw_ref, win_ref, gq_ref, gk_ref, cos_ref, sin_ref, hmean_ref, cs_ref, m1_ref,
                 qT_ref, k_ref, vT_ref, gate_ref, y_ref):
    tm = ROW_TILE
    x = x_ref[0].reshape(tm, D_MODEL)
    ms = jnp.mean(x * x, axis=-1, keepdims=True)
    h = (x * lax.rsqrt(ms + EPS) * lnw_ref[...]).astype(BF16)
    proj = jnp.dot(h, win_ref[...], preferred_element_type=F32)

    cos2 = cos_ref[...]
    sin2 = sin_ref[...]
    cos = jnp.concatenate([cos2] * (ATTN_WIDTH // KV_WIDTH), axis=1)
    sin = jnp.concatenate([sin2] * (ATTN_WIDTH // KV_WIDTH), axis=1)
    hmean = hmean_ref[...]

    q = proj[:, Q0:Q0 + ATTN_WIDTH]
    q_ms = jnp.dot((q * q).astype(BF16), hmean, preferred_element_type=F32)
    q = q * lax.rsqrt(q_ms + EPS) * gq_ref[...]
    q = _rope(q, cos, sin)
    qT = q.T.astype(BF16)
    for hd in range(N_Q_HEADS):
        j, hh = divmod(hd, GQA_GROUP)
        for t in range(tm // Q_TILE):
            qT_ref[0, j, t, :, hh * Q_TILE:(hh + 1) * Q_TILE] = (
                qT[hd * HEAD_DIM:(hd + 1) * HEAD_DIM, t * Q_TILE:(t + 1) * Q_TILE])

    k = proj[:, K0:K0 + KV_WIDTH]
    k_ms = jnp.dot((k * k).astype(BF16), hmean[:KV_WIDTH, :KV_WIDTH], preferred_element_type=F32)
    k = k * lax.rsqrt(k_ms + EPS) * gk_ref[...]
    k = _rope(k, cos2, sin2).astype(BF16)
    for j in range(N_KV_HEADS):
        k_ref[0, j] = k[:, j * HEAD_DIM:(j + 1) * HEAD_DIM]

    vT = proj[:, V0:V0 + KV_WIDTH].T.astype(BF16)
    ones = jnp.ones((VT_ROWS - HEAD_DIM, tm), BF16)
    for j in range(N_KV_HEADS):
        vT_ref[0, j, 0, :HEAD_DIM, :] = vT[j * HEAD_DIM:(j + 1) * HEAD_DIM]
        vT_ref[0, j, 0, HEAD_DIM:, :] = ones

    gate_ref[0, :, :ATTN_WIDTH] = jax.nn.silu(proj[:, GA0:GA0 + ATTN_WIDTH]).astype(BF16)
    gate_ref[0, :, ATTN_WIDTH:] = jax.nn.silu(proj[:, GF0:GF0 + FOURIER_WIDTH]).astype(BF16)

    u = proj[:, U0:U0 + FOURIER_WIDTH].astype(BF16)
    pq = jnp.dot(m1_ref[...], u, preferred_element_type=F32).astype(BF16)
    cs = cs_ref[...]
    for g in range(N_FOURIER_GROUPS):
        cols = slice(g * FOURIER_GROUP_DIM, (g + 1) * FOURIER_GROUP_DIM)
        y = jnp.dot(jnp.concatenate([pq[:tm, cols], pq[tm:, cols]], axis=1), cs, preferred_element_type=F32)
        y_ref[0, 0, 0, :, cols] = y[:, :FOURIER_GROUP_DIM].astype(BF16)
        y_ref[0, 0, 1, :, cols] = y[:, FOURIER_GROUP_DIM:].astype(BF16)


def _proj_call(x4, lnw, win, gq, gk, cos_t, sin_t, hmean, cs, m1):
    B, _, S2, _ = x4.shape
    S = DFT_A * S2
    tm = ROW_TILE
    n_t = S // tm
    const = lambda shape: pl.BlockSpec(shape, lambda b, i: (0,) * len(shape))
    return pl.pallas_call(
        _proj_kernel,
        grid=(B, n_t),
        in_specs=[
            pl.BlockSpec((1, DFT_A, TILE_J, D_MODEL), lambda b, i: (b, 0, i, 0)),
            const((1, D_MODEL)),
            const((D_MODEL, IN_WIDTH)),
            const((1, ATTN_WIDTH)),
            const((1, KV_WIDTH)),
            pl.BlockSpec((tm, KV_WIDTH), lambda b, i: (i, 0)),
            pl.BlockSpec((tm, KV_WIDTH), lambda b, i: (i, 0)),
            const((ATTN_WIDTH, ATTN_WIDTH)),
            const((2 * FOURIER_GROUP_DIM, 2 * FOURIER_GROUP_DIM)),
            const((2 * tm, tm)),
        ],
        out_specs=[
            pl.BlockSpec((1, N_KV_HEADS, tm // Q_TILE, HEAD_DIM, GQA_GROUP * Q_TILE), lambda b, i: (b, 0, i, 0, 0)),
            pl.BlockSpec((1, N_KV_HEADS, tm, HEAD_DIM), lambda b, i: (b, 0, i, 0)),
            pl.BlockSpec((1, N_KV_HEADS, 1, VT_ROWS, tm), lambda b, i: (b, 0, i, 0, 0)),
            pl.BlockSpec((1, tm, 2 * ATTN_WIDTH), lambda b, i: (b, i, 0)),
            pl.BlockSpec((1, 1, 2, tm, FOURIER_WIDTH), lambda b, i: (b, i, 0, 0, 0)),
        ],
        out_shape=[
            jax.ShapeDtypeStruct((B, N_KV_HEADS, S // Q_TILE, HEAD_DIM, GQA_GROUP * Q_TILE), BF16),
            jax.ShapeDtypeStruct((B, N_KV_HEADS, S, HEAD_DIM), BF16),
            jax.ShapeDtypeStruct((B, N_KV_HEADS, n_t, VT_ROWS, tm), BF16),
            jax.ShapeDtypeStruct((B, S, 2 * ATTN_WIDTH), BF16),
            jax.ShapeDtypeStruct((B, n_t, 2, tm, FOURIER_WIDTH), BF16),
        ],
        compiler_params=_params(("parallel", "parallel")),
        name="proj",
    )(x4, lnw, win, gq, gk, cos_t, sin_t, hmean, cs, m1)


def _attn_kernel(qT_ref, k_ref, vT_ref, o_ref, m_ref, acc_ref, s_ref, cmax_ref, *, n_chunks, tk, unroll):
    n_q = qT_ref.shape[2]
    tq = qT_ref.shape[4] // GQA_GROUP

    def scores(t, c, slot):
        off = pl.multiple_of(c * tk, tk)
        s = jnp.dot(k_ref[0, 0, pl.ds(off, tk), :], qT_ref[0, 0, t], preferred_element_type=F32)
        s_ref[slot] = s
        cmax_ref[slot, 0:1] = jnp.max(s, axis=0, keepdims=True)

    def accumulate(c, slot):
        m_old = m_ref[0:1]
        m_new = jnp.maximum(m_old, cmax_ref[slot, 0:1])
        p = jnp.exp2(s_ref[slot] - m_new).astype(BF16)
        alpha = jnp.exp2(m_old - m_new)
        pv = jnp.dot(vT_ref[0, 0, c], p, preferred_element_type=F32)
        acc_ref[...] = alpha * acc_ref[...] + pv
        m_ref[0:1] = m_new

    scores(0, 0, 0)

    def tile(t, carry):
        m_ref[0:1] = jnp.full((1, m_ref.shape[1]), -jnp.inf, F32)
        acc_ref[...] = jnp.zeros(acc_ref.shape, F32)

        def group(i, inner):
            c = unroll * i
            for u in range(unroll):
                scores(t, c + u + 1, (u + 1) % 2)
                accumulate(c + u, u % 2)
            return inner

        lax.fori_loop(0, n_chunks // unroll - 1, group, 0)
        c = n_chunks - unroll
        for u in range(unroll):
            if u < unroll - 1:
                scores(t, c + u + 1, (u + 1) % 2)
            else:
                scores(jnp.minimum(t + 1, n_q - 1), 0, 0)
            accumulate(c + u, u % 2)
        acc = acc_ref[...]
        oT = acc[:HEAD_DIM] / acc[HEAD_DIM:HEAD_DIM + 1]
        oT = jnp.concatenate([oT[:, hh * tq:(hh + 1) * tq] for hh in range(GQA_GROUP)], axis=0)
        o_ref[0, pl.ds(pl.multiple_of(t * tq, tq), tq), :] = oT.T.astype(BF16)
        return carry

    lax.fori_loop(0, n_q, tile, 0)


def _attn_bounded_kernel(qT_ref, k_ref, vT_ref, o_ref, *, n_chunks, tk):
    n_q = qT_ref.shape[2]
    tq = qT_ref.shape[4] // GQA_GROUP

    def tile(t, carry):
        acc = None
        for c in range(n_chunks):
            s = jnp.dot(k_ref[0, 0, c * tk:(c + 1) * tk, :], qT_ref[0, 0, t], preferred_element_type=F32)
            pv = jnp.dot(vT_ref[0, 0, c], jnp.exp2(s).astype(BF16), preferred_element_type=F32)
            acc = pv if acc is None else acc + pv
        oT = acc[:HEAD_DIM] / acc[HEAD_DIM:HEAD_DIM + 1]
        oT = jnp.concatenate([oT[:, hh * tq:(hh + 1) * tq] for hh in range(GQA_GROUP)], axis=0)
        o_ref[0, pl.ds(pl.multiple_of(t * tq, tq), tq), :] = oT.T.astype(BF16)
        return carry

    tiles_per_trip = max(1, BOUNDED_CHUNKS_PER_TRIP // n_chunks)
    assert n_q % tiles_per_trip == 0

    def trip(i, carry):
        for u in range(tiles_per_trip):
            tile(tiles_per_trip * i + u, carry)
        return carry

    lax.fori_loop(0, n_q // tiles_per_trip, trip, 0)


def _attn_call(qT, k, vT, bounded):
    B, _, n_q, _, _ = qT.shape
    S = k.shape[2]
    n_chunks, tk = vT.shape[2], vT.shape[4]
    unroll = 4 if n_chunks >= 16 else 2
    assert n_chunks % unroll == 0 and n_chunks >= 2 * unroll
    tq = Q_TILE
    width = GQA_GROUP * HEAD_DIM
    if bounded:
        body = functools.partial(_attn_bounded_kernel, n_chunks=n_chunks, tk=tk)
        scratch = []
    else:
        body = functools.partial(_attn_kernel, n_chunks=n_chunks, tk=tk, unroll=unroll)
        scratch = [pltpu.VMEM((SUBLANES, GQA_GROUP * tq), F32), pltpu.VMEM((VT_ROWS, GQA_GROUP * tq), F32),
                   pltpu.VMEM((2, tk, GQA_GROUP * tq), F32), pltpu.VMEM((2, SUBLANES, GQA_GROUP * tq), F32)]
    return pl.pallas_call(
        body,
        grid=(B, N_KV_HEADS),
        in_specs=[
            pl.BlockSpec((1, 1, n_q, HEAD_DIM, GQA_GROUP * tq), lambda b, j: (b, j, 0, 0, 0)),
            pl.BlockSpec((1, 1, S, HEAD_DIM), lambda b, j: (b, j, 0, 0)),
            pl.BlockSpec((1, 1, n_chunks, VT_ROWS, tk), lambda b, j: (b, j, 0, 0, 0)),
        ],
        out_specs=pl.BlockSpec((1, S, width), lambda b, j: (b, 0, j)),
        out_shape=jax.ShapeDtypeStruct((B, S, ATTN_WIDTH), BF16),
        scratch_shapes=scratch,
        compiler_params=_params(("parallel", "parallel")),
        name="attn_bounded" if bounded else "attn",
    )(qT, k, vT)


def _out_kernel(x_ref, ya_ref, gate_ref, y_ref, t_ref, r_ref, wf_ref, bf_ref, wout_ref, fn_ref, o_ref, *, scale):
    tm = ROW_TILE
    g = jnp.dot(t_ref[0], r_ref[...], preferred_element_type=F32)
    row_j = lax.broadcasted_iota(jnp.int32, g.shape, 0) % TILE_J
    col_jj = (lax.broadcasted_iota(jnp.int32, g.shape, 1) // TILE_J) % TILE_J
    g = jnp.where(row_j == col_jj, g, 0.0).astype(BF16)
    n_fold = g.shape[1]
    for bb in range(x_ref.shape[0]):
        gate_a = gate_ref[bb, :, :ATTN_WIDTH].astype(F32)
        gate_f = gate_ref[bb, :, ATTN_WIDTH:].astype(F32)
        ya = (ya_ref[bb].astype(F32) * gate_a).astype(BF16)
        y = y_ref[bb].reshape(n_fold, FOURIER_WIDTH)
        mix = (jnp.dot(g, y, preferred_element_type=F32) * scale).astype(BF16)
        parts = []
        for grp in range(N_FOURIER_GROUPS):
            lo = grp * FOURIER_GROUP_DIM
            parts.append(jnp.dot(mix[:, lo:lo + FOURIER_GROUP_DIM], wf_ref[grp], preferred_element_type=F32))
        yf = ((jnp.concatenate(parts, axis=-1) + bf_ref[...]) * gate_f).astype(BF16)
        out = x_ref[bb].reshape(tm, D_MODEL)
        out = out + jnp.dot(ya, wout_ref[:ATTN_WIDTH, :], preferred_element_type=F32)
        out = out + jnp.dot(yf, wout_ref[ATTN_WIDTH:, :], preferred_element_type=F32)
        ms = jnp.mean(out * out, axis=-1, keepdims=True)
        o_ref[bb] = (out * lax.rsqrt(ms + EPS) * fn_ref[...]).reshape(DFT_A, TILE_J, D_MODEL)


def _out_call(x4, ya, gates, y, twiddles, replicate, wf, bf, wout, fnorm):
    B, _, S2, _ = x4.shape
    S = DFT_A * S2
    tm = ROW_TILE
    n_t = S // tm
    k1_rows = TILE_J * TILE_J
    n_k1_blocks = DFT_A // TILE_J
    nb = math.gcd(B, OUT_BATCH)
    const = lambda shape: pl.BlockSpec(shape, lambda i, b: (0,) * len(shape))
    scale = 1.0 / math.sqrt(S * FOURIER_GROUP_DIM)
    return pl.pallas_call(
        functools.partial(_out_kernel, scale=scale),
        grid=(n_t, B // nb),
        in_specs=[
            pl.BlockSpec((nb, DFT_A, TILE_J, D_MODEL), lambda i, b: (b, 0, i, 0)),
            pl.BlockSpec((nb, tm, ATTN_WIDTH), lambda i, b: (b, i, 0)),
            pl.BlockSpec((nb, tm, 2 * ATTN_WIDTH), lambda i, b: (b, i, 0)),
            pl.BlockSpec((nb, n_t, 2, k1_rows, FOURIER_WIDTH), lambda i, b: (b, 0, 0, i % n_k1_blocks, 0)),
            pl.BlockSpec((1, tm, twiddles.shape[2]), lambda i, b: (i, 0, 0)),
            const(replicate.shape),
            const((N_FOURIER_GROUPS, FOURIER_GROUP_DIM, FOURIER_GROUP_DIM)),
            const((1, FOURIER_WIDTH)),
            const((D_MODEL, D_MODEL)),
            const((1, D_MODEL)),
        ],
        out_specs=pl.BlockSpec((nb, DFT_A, TILE_J, D_MODEL), lambda i, b: (b, 0, i, 0)),
        out_shape=jax.ShapeDtypeStruct((B, DFT_A, S2, D_MODEL), F32),
        compiler_params=_params(("parallel", "parallel")),
        name="out_proj",
    )(x4, ya, gates, y, twiddles, replicate, wf, bf, wout, fnorm)


def _tile_positions(seq_len):
    s2 = seq_len // DFT_A
    i = np.arange(s2 // TILE_J)[:, None, None]
    a = np.arange(DFT_A)[None, :, None]
    j = np.arange(TILE_J)[None, None, :]
    return (a * s2 + TILE_J * i + j).reshape(-1)


def _rope_tables(seq_len):
    pos = _tile_positions(seq_len)
    axis_dim = HEAD_DIM // 2
    inv_freq = ROPE_THETA ** (-np.arange(0, axis_dim, 2, dtype=np.float64) / axis_dim)
    row = (pos // GRID_W).astype(np.float64)
    col = (pos % GRID_W).astype(np.float64)
    ang = np.concatenate([row[:, None] * inv_freq, col[:, None] * inv_freq], axis=-1)
    cos, sin = np.cos(ang), np.sin(ang)
    cos_pair = np.repeat(cos, 2, axis=-1)
    sin_pair = np.stack([-sin, sin], axis=-1).reshape(seq_len, HEAD_DIM)
    tables = [np.tile(t, (1, N_KV_HEADS)).astype(np.float32) for t in (cos_pair, sin_pair)]
    return jnp.asarray(tables[0]), jnp.asarray(tables[1])


def _dft_cos_sin(n):
    idx = np.arange(n, dtype=np.int64)
    ang = ((idx[:, None] * idx[None, :]) % n).astype(np.float64) * (2.0 * math.pi / n)
    return np.cos(ang), np.sin(ang)


def _stage1_matrix():
    c, s = _dft_cos_sin(DFT_A)
    eye = np.eye(TILE_J)
    stacked = np.concatenate([np.kron(c, eye), np.kron(s, eye)], axis=0)
    return jnp.asarray(stacked.astype(np.float32)).astype(BF16)


def _stage2_twiddles(seq_len):
    s2_len = seq_len // DFT_A
    n_t = s2_len // TILE_J
    k = _tile_positions(seq_len).astype(np.int64)
    s2 = np.arange(s2_len, dtype=np.int64)
    ang = ((k[:, None] * s2[None, :]) % seq_len).astype(np.float64) * (2.0 * math.pi / seq_len)
    trig = np.stack([np.cos(ang), np.sin(ang)], axis=1)
    trig = trig.reshape(n_t, ROW_TILE, 2, n_t, TILE_J).transpose(0, 1, 3, 2, 4)
    n_in = n_t * 2 * TILE_J
    n_out = n_in * TILE_J
    compact = jnp.asarray(trig.reshape(n_t, ROW_TILE, n_in).astype(np.float32)).astype(BF16)
    copy_j = np.kron(np.ones((1, TILE_J)), np.eye(TILE_J))
    replicate = jnp.asarray(np.kron(np.eye(n_t * 2), copy_j), BF16)
    return compact, replicate


def _trunk(x, score_bound, lnw, win, gq, gk, hmean, cs, m1, wf, bf, wout, fnorm):
    B, S, _ = x.shape
    assert S % ROW_TILE == 0
    s2 = S // DFT_A
    x4 = x.reshape(B, DFT_A, s2, D_MODEL)
    cos_t, sin_t = _rope_tables(S)
    qT, k, vT, gates, y = _proj_call(x4, lnw, win, gq, gk, cos_t, sin_t, hmean, cs, m1)
    ya = lax.cond(score_bound <= SCORE_BOUND_LOG2,
                  functools.partial(_attn_call, bounded=True),
                  functools.partial(_attn_call, bounded=False), qT, k, vT)
    twiddles, replicate = _stage2_twiddles(S)
    out = _out_call(x4, ya, gates, y, twiddles, replicate, wf, bf, wout, fnorm)
    return out.reshape(B, S, D_MODEL)


def kernel(x_prompt, x_sample, ln_w, w_in, q_norm, k_norm, w_fourier, b_fourier, w_out, final_norm):
    assert ln_w.shape[0] == 1, "single mixer layer"
    lnw = ln_w[0].reshape(1, D_MODEL)
    win = w_in[0].astype(BF16)
    q_scale = HEAD_DIM ** -0.5 * math.log2(math.e)
    gq = jnp.tile(q_norm[0] * q_scale, N_Q_HEADS).reshape(1, ATTN_WIDTH)
    gk = jnp.tile(k_norm[0], N_KV_HEADS).reshape(1, KV_WIDTH)
    score_bound = HEAD_DIM * jnp.max(jnp.abs(q_norm[0] * q_scale)) * jnp.max(jnp.abs(k_norm[0]))
    head_id = np.arange(ATTN_WIDTH) // HEAD_DIM
    hmean = jnp.asarray((head_id[:, None] == head_id[None, :]) / HEAD_DIM, BF16)
    c_c, s_c = _dft_cos_sin(FOURIER_GROUP_DIM)
    cs = jnp.asarray(np.block([[c_c, -s_c], [-s_c, -c_c]]).astype(np.float32)).astype(BF16)
    m1 = _stage1_matrix()
    wf = w_fourier[0].astype(BF16)
    bf = b_fourier[0].reshape(1, FOURIER_WIDTH)
    wout = w_out[0].astype(BF16)
    fnorm = final_norm.reshape(1, D_MODEL)
    args = (lnw, win, gq, gk, hmean, cs, m1, wf, bf, wout, fnorm)
    return (_trunk(x_prompt, score_bound, *args), _trunk(x_sample, score_bound, *args))
```

```python
import functools
import math

import jax
import jax.numpy as jnp
import numpy as np
from jax import lax
from jax.experimental import pallas as pl
from jax.experimental.pallas import tpu as pltpu

D_MODEL = 1024
GRID_W = 64
ATTN_WIDTH = 512
FOURIER_WIDTH = 512
HEAD_DIM = 64
N_Q_HEADS = 8
N_KV_HEADS = 2
GQA_GROUP = 4
KV_WIDTH = 128
N_FOURIER_GROUPS = 4
FOURIER_GROUP_DIM = 128
ROPE_THETA = 10000.0
EPS = 1e-6
IN_WIDTH = 2304
Q0, K0, V0, GA0, U0, GF0 = 0, 512, 640, 768, 1280, 1792

VT_ROWS = HEAD_DIM + 16

DFT_A = 64
SUBLANES = 8
TILE_J = SUBLANES
ROW_TILE = DFT_A * TILE_J
Q_TILE = 512
PROJ_BATCH = 2
OUT_BATCH = 4
VMEM_LIMIT_BYTES = 60 * 1024 * 1024
SCORE_BOUND_LOG2 = 64.0
BOUNDED_CHUNKS_PER_TRIP = 16

F32 = jnp.float32
BF16 = jnp.bfloat16


def _params(semantics):
    return pltpu.CompilerParams(dimension_semantics=semantics, vmem_limit_bytes=VMEM_LIMIT_BYTES)


def _rope(x, cos, sin_signed):
    n = x.shape[-1]
    lane_is_even = (lax.broadcasted_iota(jnp.int32, x.shape, 1) & 1) == 0
    partner = jnp.where(lane_is_even, pltpu.roll(x, n - 1, 1), pltpu.roll(x, 1, 1))
    return x * cos + partner * sin_signed


def _proj_kernel(x_ref, lnw_ref, win_ref, gq_ref, gk_ref, cos_ref, sin_ref, hmean_ref, cs_ref, m1_ref,
                 qT_ref, k_ref, vT_ref, gate_ref, y_ref):
    for bb in range(x_ref.shape[0]):
        one = lambda ref: ref.at[pl.ds(bb, 1)]
        _proj_tile(one(x_ref), lnw_ref, win_ref, gq_ref, gk_ref, cos_ref, sin_ref, hmean_ref, cs_ref, m1_ref,
                   one(qT_ref), one(k_ref), one(vT_ref), one(gate_ref), one(y_ref))


def _proj_tile(x_ref, lnw_ref, win_ref, gq_ref, gk_ref, cos_ref, sin_ref, hmean_ref, cs_ref, m1_ref,
               qT_ref, k_ref, vT_ref, gate_ref, y_ref):
    tm = ROW_TILE
    x = x_ref[0].reshape(tm, D_MODEL)
    ms = jnp.mean(x * x, axis=-1, keepdims=True)
    h = (x * lax.rsqrt(ms + EPS) * lnw_ref[...]).astype(BF16)
    proj = jnp.dot(h, win_ref[...], preferred_element_type=F32)

    cos2 = cos_ref[...]
    sin2 = sin_ref[...]
    cos = jnp.concatenate([cos2] * (ATTN_WIDTH // KV_WIDTH), axis=1)
    sin = jnp.concatenate([sin2] * (ATTN_WIDTH // KV_WIDTH), axis=1)
    hmean = hmean_ref[...]

    q = proj[:, Q0:Q0 + ATTN_WIDTH]
    q_ms = jnp.dot((q * q).astype(BF16), hmean, preferred_element_type=F32)
    q = q * lax.rsqrt(q_ms + EPS) * gq_ref[...]
    q = _rope(q, cos, sin)
    qT = q.T.astype(BF16)
    for hd in range(N_Q_HEADS):
        j, hh = divmod(hd, GQA_GROUP)
        for t in range(tm // Q_TILE):
            qT_ref[0, j, t, :, hh * Q_TILE:(hh + 1) * Q_TILE] = (
                qT[hd * HEAD_DIM:(hd + 1) * HEAD_DIM, t * Q_TILE:(t + 1) * Q_TILE])

    k = proj[:, K0:K0 + KV_WIDTH]
    k_ms = jnp.dot((k * k).astype(BF16), hmean[:KV_WIDTH, :KV_WIDTH], preferred_element_type=F32)
    k = k * lax.rsqrt(k_ms + EPS) * gk_ref[...]
    k = _rope(k, cos2, sin2).astype(BF16)
    for j in range(N_KV_HEADS):
        k_ref[0, j] = k[:, j * HEAD_DIM:(j + 1) * HEAD_DIM]

    vT = proj[:, V0:V0 + KV_WIDTH].T.astype(BF16)
    ones = jnp.ones((VT_ROWS - HEAD_DIM, tm), BF16)
    for j in range(N_KV_HEADS):
        vT_ref[0, j, 0, :HEAD_DIM, :] = vT[j * HEAD_DIM:(j + 1) * HEAD_DIM]
        vT_ref[0, j, 0, HEAD_DIM:, :] = ones

    gate_ref[0, :, :ATTN_WIDTH] = jax.nn.silu(proj[:, GA0:GA0 + ATTN_WIDTH]).astype(BF16)
    gate_ref[0, :, ATTN_WIDTH:] = jax.nn.silu(proj[:, GF0:GF0 + FOURIER_WIDTH]).astype(BF16)

    u = proj[:, U0:U0 + FOURIER_WIDTH].astype(BF16)
    pq = jnp.dot(m1_ref[...], u, preferred_element_type=F32).astype(BF16)
    cs = cs_ref[...]
    for g in range(N_FOURIER_GROUPS):
        cols = slice(g * FOURIER_GROUP_DIM, (g + 1) * FOURIER_GROUP_DIM)
        y = jnp.dot(jnp.concatenate([pq[:tm, cols], pq[tm:, cols]], axis=1), cs, preferred_element_type=F32)
        y_ref[0, 0, 0, :, cols] = y[:, :FOURIER_GROUP_DIM].astype(BF16)
        y_ref[0, 0, 1, :, cols] = y[:, FOURIER_GROUP_DIM:].astype(BF16)


def _proj_call(x4, lnw, win, gq, gk, cos_t, sin_t, hmean, cs, m1):
    B, _, S2, _ = x4.shape
    S = DFT_A * S2
    tm = ROW_TILE
    n_t = S // tm
    nb = math.gcd(B, PROJ_BATCH)
    const = lambda shape: pl.BlockSpec(shape, lambda b, i: (0,) * len(shape))
    return pl.pallas_call(
        _proj_kernel,
        grid=(B // nb, n_t),
        in_specs=[
            pl.BlockSpec((nb, DFT_A, TILE_J, D_MODEL), lambda b, i: (b, 0, i, 0)),
            const((1, D_MODEL)),
            const((D_MODEL, IN_WIDTH)),
            const((1, ATTN_WIDTH)),
            const((1, KV_WIDTH)),
            pl.BlockSpec((tm, KV_WIDTH), lambda b, i: (i, 0)),
            pl.BlockSpec((tm, KV_WIDTH), lambda b, i: (i, 0)),
            const((ATTN_WIDTH, ATTN_WIDTH)),
            const((2 * FOURIER_GROUP_DIM, 2 * FOURIER_GROUP_DIM)),
            const((2 * tm, tm)),
        ],
        out_specs=[
            pl.BlockSpec((nb, N_KV_HEADS, tm // Q_TILE, HEAD_DIM, GQA_GROUP * Q_TILE), lambda b, i: (b, 0, i, 0, 0)),
            pl.BlockSpec((nb, N_KV_HEADS, tm, HEAD_DIM), lambda b, i: (b, 0, i, 0)),
            pl.BlockSpec((nb, N_KV_HEADS, 1, VT_ROWS, tm), lambda b, i: (b, 0, i, 0, 0)),
            pl.BlockSpec((nb, tm, 2 * ATTN_WIDTH), lambda b, i: (b, i, 0)),
            pl.BlockSpec((nb, 1, 2, tm, FOURIER_WIDTH), lambda b, i: (b, i, 0, 0, 0)),
        ],
        out_shape=[
            jax.ShapeDtypeStruct((B, N_KV_HEADS, S // Q_TILE, HEAD_DIM, GQA_GROUP * Q_TILE), BF16),
            jax.ShapeDtypeStruct((B, N_KV_HEADS, S, HEAD_DIM), BF16),
            jax.ShapeDtypeStruct((B, N_KV_HEADS, n_t, VT_ROWS, tm), BF16),
            jax.ShapeDtypeStruct((B, S, 2 * ATTN_WIDTH), BF16),
            jax.ShapeDtypeStruct((B, n_t, 2, tm, FOURIER_WIDTH), BF16),
        ],
        compiler_params=_params(("parallel", "parallel")),
        name="proj",
    )(x4, lnw, win, gq, gk, cos_t, sin_t, hmean, cs, m1)


def _attn_kernel(qT_ref, k_ref, vT_ref, o_ref, m_ref, acc_ref, s_ref, cmax_ref, *, n_chunks, tk, unroll):
    n_q = qT_ref.shape[2]
    tq = qT_ref.shape[4] // GQA_GROUP

    def scores(t, c, slot):
        off = pl.multiple_of(c * tk, tk)
        s = jnp.dot(k_ref[0, 0, pl.ds(off, tk), :], qT_ref[0, 0, t], preferred_element_type=F32)
        s_ref[slot] = s
        cmax_ref[slot, 0:1] = jnp.max(s, axis=0, keepdims=True)

    def accumulate(c, slot):
        m_old = m_ref[0:1]
        m_new = jnp.maximum(m_old, cmax_ref[slot, 0:1])
        p = jnp.exp2(s_ref[slot] - m_new).astype(BF16)
        alpha = jnp.exp2(m_old - m_new)
        pv = jnp.dot(vT_ref[0, 0, c], p, preferred_element_type=F32)
        acc_ref[...] = alpha * acc_ref[...] + pv
        m_ref[0:1] = m_new

    scores(0, 0, 0)

    def tile(t, carry):
        m_ref[0:1] = jnp.full((1, m_ref.shape[1]), -jnp.inf, F32)
        acc_ref[...] = jnp.zeros(acc_ref.shape, F32)

        def group(i, inner):
            c = unroll * i
            for u in range(unroll):
                scores(t, c + u + 1, (u + 1) % 2)
                accumulate(c + u, u % 2)
            return inner

        lax.fori_loop(0, n_chunks // unroll - 1, group, 0)
        c = n_chunks - unroll
        for u in range(unroll):
            if u < unroll - 1:
                scores(t, c + u + 1, (u + 1) % 2)
            else:
                scores(jnp.minimum(t + 1, n_q - 1), 0, 0)
            accumulate(c + u, u % 2)
        acc = acc_ref[...]
        oT = acc[:HEAD_DIM] / acc[HEAD_DIM:HEAD_DIM + 1]
        oT = jnp.concatenate([oT[:, hh * tq:(hh + 1) * tq] for hh in range(GQA_GROUP)], axis=0)
        o_ref[0, pl.ds(pl.multiple_of(t * tq, tq), tq), :] = oT.T.astype(BF16)
        return carry

    lax.fori_loop(0, n_q, tile, 0)


def _attn_bounded_kernel(qT_ref, k_ref, vT_ref, o_ref, *, n_chunks, tk):
    n_q = qT_ref.shape[2]
    tq = qT_ref.shape[4] // GQA_GROUP

    def tile(t, carry):
        acc = None
        for c in range(n_chunks):
            s = jnp.dot(k_ref[0, 0, c * tk:(c + 1) * tk, :], qT_ref[0, 0, t], preferred_element_type=F32)
            pv = jnp.dot(vT_ref[0, 0, c], jnp.exp2(s).astype(BF16), preferred_element_type=F32)
            acc = pv if acc is None else acc + pv
        oT = acc[:HEAD_DIM] / acc[HEAD_DIM:HEAD_DIM + 1]
        oT = jnp.concatenate([oT[:, hh * tq:(hh + 1) * tq] for hh in range(GQA_GROUP)], axis=0)
        o_ref[0, pl.ds(pl.multiple_of(t * tq, tq), tq), :] = oT.T.astype(BF16)
        return carry

    tiles_per_trip = max(1, BOUNDED_CHUNKS_PER_TRIP // n_chunks)
    assert n_q % tiles_per_trip == 0

    def trip(i, carry):
        for u in range(tiles_per_trip):
            tile(tiles_per_trip * i + u, carry)
        return carry

    lax.fori_loop(0, n_q // tiles_per_trip, trip, 0)


def _attn_call(qT, k, vT, bounded):
    B, _, n_q, _, _ = qT.shape
    S = k.shape[2]
    n_chunks, tk = vT.shape[2], vT.shape[4]
    unroll = 4 if n_chunks >= 16 else 2
    assert n_chunks % unroll == 0 and n_chunks >= 2 * unroll
    tq = Q_TILE
    width = GQA_GROUP * HEAD_DIM
    if bounded:
        body = functools.partial(_attn_bounded_kernel, n_chunks=n_chunks, tk=tk)
        scratch = []
    else:
        body = functools.partial(_attn_kernel, n_chunks=n_chunks, tk=tk, unroll=unroll)
        scratch = [pltpu.VMEM((SUBLANES, GQA_GROUP * tq), F32), pltpu.VMEM((VT_ROWS, GQA_GROUP * tq), F32),
                   pltpu.VMEM((2, tk, GQA_GROUP * tq), F32), pltpu.VMEM((2, SUBLANES, GQA_GROUP * tq), F32)]
    return pl.pallas_call(
        body,
        grid=(B, N_KV_HEADS),
        in_specs=[
            pl.BlockSpec((1, 1, n_q, HEAD_DIM, GQA_GROUP * tq), lambda b, j: (b, j, 0, 0, 0)),
            pl.BlockSpec((1, 1, S, HEAD_DIM), lambda b, j: (b, j, 0, 0)),
            pl.BlockSpec((1, 1, n_chunks, VT_ROWS, tk), lambda b, j: (b, j, 0, 0, 0)),
        ],
        out_specs=pl.BlockSpec((1, S, width), lambda b, j: (b, 0, j)),
        out_shape=jax.ShapeDtypeStruct((B, S, ATTN_WIDTH), BF16),
        scratch_shapes=scratch,
        compiler_params=_params(("parallel", "parallel")),
        name="attn_bounded" if bounded else "attn",
    )(qT, k, vT)


def _out_kernel(x_ref, ya_ref, gate_ref, y_ref, t_ref, r_ref, wf_ref, bf_ref, wout_ref, fn_ref, o_ref, *, scale):
    tm = ROW_TILE
    g = jnp.dot(t_ref[0], r_ref[...], preferred_element_type=F32)
    row_j = lax.broadcasted_iota(jnp.int32, g.shape, 0) % TILE_J
    col_jj = (lax.broadcasted_iota(jnp.int32, g.shape, 1) // TILE_J) % TILE_J
    g = jnp.where(row_j == col_jj, g, 0.0).astype(BF16)
    n_fold = g.shape[1]
    for bb in range(x_ref.shape[0]):
        gate_a = gate_ref[bb, :, :ATTN_WIDTH].astype(F32)
        gate_f = gate_ref[bb, :, ATTN_WIDTH:].astype(F32)
        ya = (ya_ref[bb].astype(F32) * gate_a).astype(BF16)
        y = y_ref[bb].reshape(n_fold, FOURIER_WIDTH)
        mix = (jnp.dot(g, y, preferred_element_type=F32) * scale).astype(BF16)
        parts = []
        for grp in range(N_FOURIER_GROUPS):
            lo = grp * FOURIER_GROUP_DIM
            parts.append(jnp.dot(mix[:, lo:lo + FOURIER_GROUP_DIM], wf_ref[grp], preferred_element_type=F32))
        yf = ((jnp.concatenate(parts, axis=-1) + bf_ref[...]) * gate_f).astype(BF16)
        out = x_ref[bb].reshape(tm, D_MODEL)
        out = out + jnp.dot(ya, wout_ref[:ATTN_WIDTH, :], preferred_element_type=F32)
        out = out + jnp.dot(yf, wout_ref[ATTN_WIDTH:, :], preferred_element_type=F32)
        ms = jnp.mean(out * out, axis=-1, keepdims=True)
        o_ref[bb] = (out * lax.rsqrt(ms + EPS) * fn_ref[...]).reshape(DFT_A, TILE_J, D_MODEL)


def _out_call(x4, ya, gates, y, twiddles, replicate, wf, bf, wout, fnorm):
    B, _, S2, _ = x4.shape
    S = DFT_A * S2
    tm = ROW_TILE
    n_t = S // tm
    k1_rows = TILE_J * TILE_J
    n_k1_blocks = DFT_A // TILE_J
    nb = math.gcd(B, OUT_BATCH)
    const = lambda shape: pl.BlockSpec(shape, lambda i, b: (0,) * len(shape), pipeline_mode=pl.Buffered(1))
    scale = 1.0 / math.sqrt(S * FOURIER_GROUP_DIM)
    return pl.pallas_call(
        functools.partial(_out_kernel, scale=scale),
        grid=(n_t, B // nb),
        in_specs=[
            pl.BlockSpec((nb, DFT_A, TILE_J, D_MODEL), lambda i, b: (b, 0, i, 0)),
            pl.BlockSpec((nb, tm, ATTN_WIDTH), lambda i, b: (b, i, 0)),
            pl.BlockSpec((nb, tm, 2 * ATTN_WIDTH), lambda i, b: (b, i, 0)),
            pl.BlockSpec((nb, n_t, 2, k1_rows, FOURIER_WIDTH), lambda i, b: (b, 0, 0, i % n_k1_blocks, 0)),
            pl.BlockSpec((1, tm, twiddles.shape[2]), lambda i, b: (i, 0, 0)),
            const(replicate.shape),
            const((N_FOURIER_GROUPS, FOURIER_GROUP_DIM, FOURIER_GROUP_DIM)),
            const((1, FOURIER_WIDTH)),
            const((D_MODEL, D_MODEL)),
            const((1, D_MODEL)),
        ],
        out_specs=pl.BlockSpec((nb, DFT_A, TILE_J, D_MODEL), lambda i, b: (b, 0, i, 0)),
        out_shape=jax.ShapeDtypeStruct((B, DFT_A, S2, D_MODEL), F32),
        compiler_params=_params(("parallel", "parallel")),
        name="out_proj",
    )(x4, ya, gates, y, twiddles, replicate, wf, bf, wout, fnorm)


def _tile_positions(seq_len):
    s2 = seq_len // DFT_A
    i = np.arange(s2 // TILE_J)[:, None, None]
    a = np.arange(DFT_A)[None, :, None]
    j = np.arange(TILE_J)[None, None, :]
    return (a * s2 + TILE_J * i + j).reshape(-1)


def _rope_tables(seq_len):
    pos = _tile_positions(seq_len)
    axis_dim = HEAD_DIM // 2
    inv_freq = ROPE_THETA ** (-np.arange(0, axis_dim, 2, dtype=np.float64) / axis_dim)
    row = (pos // GRID_W).astype(np.float64)
    col = (pos % GRID_W).astype(np.float64)
    ang = np.concatenate([row[:, None] * inv_freq, col[:, None] * inv_freq], axis=-1)
    cos, sin = np.cos(ang), np.sin(ang)
    cos_pair = np.repeat(cos, 2, axis=-1)
    sin_pair = np.stack([-sin, sin], axis=-1).reshape(seq_len, HEAD_DIM)
    tables = [np.tile(t, (1, N_KV_HEADS)).astype(np.float32) for t in (cos_pair, sin_pair)]
    return jnp.asarray(tables[0]), jnp.asarray(tables[1])


def _dft_cos_sin(n):
    idx = np.arange(n, dtype=np.int64)
    ang = ((idx[:, None] * idx[None, :]) % n).astype(np.float64) * (2.0 * math.pi / n)
    return np.cos(ang), np.sin(ang)


def _stage1_matrix():
    c, s = _dft_cos_sin(DFT_A)
    eye = np.eye(TILE_J)
    stacked = np.concatenate([np.kron(c, eye), np.kron(s, eye)], axis=0)
    return jnp.asarray(stacked.astype(np.float32)).astype(BF16)


def _stage2_twiddles(seq_len):
    s2_len = seq_len // DFT_A
    n_t = s2_len // TILE_J
    k = _tile_positions(seq_len).astype(np.int64)
    s2 = np.arange(s2_len, dtype=np.int64)
    ang = ((k[:, None] * s2[None, :]) % seq_len).astype(np.float64) * (2.0 * math.pi / seq_len)
    trig = np.stack([np.cos(ang), np.sin(ang)], axis=1)
    trig = trig.reshape(n_t, ROW_TILE, 2, n_t, TILE_J).transpose(0, 1, 3, 2, 4)
    n_in = n_t * 2 * TILE_J
    n_out = n_in * TILE_J
    compact = jnp.asarray(trig.reshape(n_t, ROW_TILE, n_in).astype(np.float32)).astype(BF16)
    copy_j = np.kron(np.ones((1, TILE_J)), np.eye(TILE_J))
    replicate = jnp.asarray(np.kron(np.eye(n_t * 2), copy_j), BF16)
    return compact, replicate


def _trunk(x, score_bound, lnw, win, gq, gk, hmean, cs, m1, wf, bf, wout, fnorm):
    B, S, _ = x.shape
    assert S % ROW_TILE == 0
    s2 = S // DFT_A
    x4 = x.reshape(B, DFT_A, s2, D_MODEL)
    cos_t, sin_t = _rope_tables(S)
    qT, k, vT, gates, y = _proj_call(x4, lnw, win, gq, gk, cos_t, sin_t, hmean, cs, m1)
    ya = lax.cond(score_bound <= SCORE_BOUND_LOG2,
                  functools.partial(_attn_call, bounded=True),
                  functools.partial(_attn_call, bounded=False), qT, k, vT)
    twiddles, replicate = _stage2_twiddles(S)
    out = _out_call(x4, ya, gates, y, twiddles, replicate, wf, bf, wout, fnorm)
    return out.reshape(B, S, D_MODEL)


def kernel(x_prompt, x_sample, ln_w, w_in, q_norm, k_norm, w_fourier, b_fourier, w_out, final_norm):
    assert ln_w.shape[0] == 1, "single mixer layer"
    lnw = ln_w[0].reshape(1, D_MODEL)
    win = w_in[0].astype(BF16)
    q_scale = HEAD_DIM ** -0.5 * math.log2(math.e)
    gq = jnp.tile(q_norm[0] * q_scale, N_Q_HEADS).reshape(1, ATTN_WIDTH)
    gk = jnp.tile(k_norm[0], N_KV_HEADS).reshape(1, KV_WIDTH)
    score_bound = HEAD_DIM * jnp.max(jnp.abs(q_norm[0] * q_scale)) * jnp.max(jnp.abs(k_norm[0]))
    head_id = np.arange(ATTN_WIDTH) // HEAD_DIM
    hmean = jnp.asarray((head_id[:, None] == head_id[None, :]) / HEAD_DIM, BF16)
    c_c, s_c = _dft_cos_sin(FOURIER_GROUP_DIM)
    cs = jnp.asarray(np.block([[c_c, -s_c], [-s_c, -c_c]]).astype(np.float32)).astype(BF16)
    m1 = _stage1_matrix()
    wf = w_fourier[0].astype(BF16)
    bf = b_fourier[0].reshape(1, FOURIER_WIDTH)
    wout = w_out[0].astype(BF16)
    fnorm = final_norm.reshape(1, D_MODEL)
    args = (lnw, win, gq, gk, hmean, cs, m1, wf, bf, wout, fnorm)
    return (_trunk(x_prompt, score_bound, *args), _trunk(x_sample, score_bound, *args))
```
